```python
import math
import jax, jax.numpy as jnp
from jax import lax
import numpy as np

D_MODEL = 1024
BATCH = 8
SEQ = 4096
DEPTH = 2

D_MIX = D_MODEL
HEAD_DIM = 64
D_SSD = D_MIX // 2
D_ATT = D_MIX - D_SSD
N_SSD_HEADS = D_SSD // HEAD_DIM
N_ATT_HEADS = D_ATT // HEAD_DIM
SSD_GROUPS = 2
D_STATE = 128
CONV_K = 4
CHUNK = 128
Q_BLOCK = 128
D_CONV = D_SSD + 2 * SSD_GROUPS * D_STATE
D_IN_SSD = D_SSD + D_CONV + N_SSD_HEADS
D_IN_ATT = 3 * D_ATT + N_ATT_HEADS
D_IN = D_IN_SSD + D_IN_ATT
D_FF_DENSE = 11 * D_MODEL // 4
N_EXPERTS = 8
TOP_K = 2
D_FF_EXPERT = D_FF_DENSE // 2
PLE_DIM = 256
N_DENSE = (DEPTH + 1) // 2
N_MOE = DEPTH // 2
EPS = 1e-6

kernel_name = 'hybrid_ssd_fox_moe_ple_block'


def rmsnorm(x, g):
    xf = x.astype(jnp.float32)
    y = xf * lax.rsqrt(jnp.mean(xf * xf, axis=-1, keepdims=True) + EPS)
    return (y * g.astype(jnp.float32)).astype(x.dtype)


def causal_depthwise_conv(u, w, b):
    c = u.shape[-1]
    y = lax.conv_general_dilated(
        u, w[:, None, :].astype(u.dtype), window_strides=(1,),
        padding=[(w.shape[0] - 1, 0)], dimension_numbers=('NWC', 'WIO', 'NWC'),
        feature_group_count=c)
    return y + b.astype(u.dtype)


def segsum(a):
    t = a.shape[-1]
    c = jnp.cumsum(a, axis=-1)
    d = c[..., :, None] - c[..., None, :]
    return jnp.where(jnp.tril(jnp.ones((t, t), dtype=bool)), d, -jnp.inf)


def ssd_chunked(xh, dt, a, bm, cm):
    b, s, h, pdim = xh.shape
    nc = s // CHUNK
    rep = h // bm.shape[2]
    bm = jnp.repeat(bm, rep, axis=2).reshape(b, nc, CHUNK, h, D_STATE)
    cm = jnp.repeat(cm, rep, axis=2).reshape(b, nc, CHUNK, h, D_STATE)
    xdt = (xh * dt[..., None]).reshape(b, nc, CHUNK, h, pdim)
    adt = (dt * a).reshape(b, nc, CHUNK, h).transpose(0, 3, 1, 2)
    a_cum = jnp.cumsum(adt, axis=-1)
    decay_in = jnp.exp(segsum(adt))
    y_diag = jnp.einsum('bclhn,bcshn,bhcls,bcshp->bclhp', cm, bm, decay_in, xdt)
    decay_to_end = jnp.exp(a_cum[..., -1:] - a_cum)
    states = jnp.einsum('bclhn,bhcl,bclhp->bchpn', bm, decay_to_end, xdt)
    chunk_decay = jnp.exp(a_cum[..., -1])

    def step(hstate, inp):
        s_c, d_c = inp
        return hstate * d_c[..., None, None] + s_c, hstate

    h0 = jnp.zeros((b, h, pdim, D_STATE), jnp.float32)
    _, prev = lax.scan(step, h0, (states.transpose(1, 0, 2, 3, 4), chunk_decay.transpose(2, 0, 1)))
    prev = prev.transpose(1, 0, 2, 3, 4)
    y_off = jnp.einsum('bclhn,bchpn,bhcl->bclhp', cm, prev, jnp.exp(a_cum))
    return (y_diag + y_off).reshape(b, s, h, pdim)


def forgetting_attention(q, k, v, log_f):
    b, s, h, d = q.shape
    nb = s // Q_BLOCK
    f_cum = jnp.cumsum(log_f.astype(jnp.float32), axis=1).transpose(0, 2, 1)
    q_blocks = q.reshape(b, nb, Q_BLOCK, h, d).transpose(1, 0, 2, 3, 4)
    fq_blocks = f_cum.reshape(b, h, nb, Q_BLOCK).transpose(2, 0, 1, 3)
    key_pos = jnp.arange(s)
    scale = d ** -0.5

    def one_block(args):
        i, q_i, fq_i = args
        logits = jnp.einsum('bqhd,bkhd->bhqk', q_i, k).astype(jnp.float32) * scale
        logits = logits + fq_i[..., :, None] - f_cum[..., None, :]
        q_pos = i * Q_BLOCK + jnp.arange(Q_BLOCK)
        logits = jnp.where(q_pos[:, None] >= key_pos[None, :], logits, -jnp.inf)
        probs = jax.nn.softmax(logits, axis=-1)
        return jnp.einsum('bhqk,bkhd->bqhd', probs.astype(v.dtype), v)

    out = lax.map(one_block, (jnp.arange(nb), q_blocks, fq_blocks))
    return out.transpose(1, 0, 2, 3, 4).reshape(b, s, h, d)


def swiglu(t, wg, wu, wd):
    return (jax.nn.silu(t @ wg) * (t @ wu)) @ wd


def moe_swiglu(h, w_router, wg, wu, wd):
    b, s, d = h.shape
    t = h.reshape(b * s, d)
    logits = (t @ w_router).astype(jnp.float32)
    top_v, top_i = lax.top_k(logits, TOP_K)
    gates = jax.nn.softmax(top_v, axis=-1)
    combine = jnp.sum(jax.nn.one_hot(top_i, N_EXPERTS, dtype=jnp.float32) * gates[..., None], axis=1)
    out = jnp.zeros_like(t)
    for e in range(N_EXPERTS):
        out = out + combine[:, e:e + 1].astype(t.dtype) * swiglu(t, wg[e], wu[e], wd[e])
    return out.reshape(b, s, d)


def setup_inputs(seed: int = 0) -> dict:
    key = jax.random.key(seed)
    ks = jax.random.split(key, 32)
    f32 = jnp.float32

    def nrm(k, shape, fan_in):
        return jax.random.normal(k, shape, f32) * (fan_in ** -0.5)

    def gain(k, shape):
        return 1.0 + 0.05 * jax.random.normal(k, shape, f32)

    dt0 = jnp.exp(jax.random.uniform(ks[6], (DEPTH, N_SSD_HEADS), f32,
                                     minval=math.log(1e-3), maxval=math.log(1e-1)))
    return {
        'x': jax.random.normal(ks[0], (BATCH, SEQ, D_MODEL), f32),
        'p': jax.random.normal(ks[1], (DEPTH, BATCH, SEQ, PLE_DIM), f32),
        'norm1_g': gain(ks[2], (DEPTH, D_MODEL)),
        'w_in': nrm(ks[3], (DEPTH, D_MODEL, D_IN), D_MODEL),
        'conv_w': nrm(ks[4], (DEPTH, CONV_K, D_CONV), CONV_K),
        'conv_b': 0.01 * jax.random.normal(ks[5], (DEPTH, D_CONV), f32),
        'dt_bias': dt0 + jnp.log(-jnp.expm1(-dt0)),
        'a_log': jnp.log(jax.random.uniform(ks[7], (DEPTH, N_SSD_HEADS), f32, minval=1.0, maxval=16.0)),
        'd_skip': gain(ks[8], (DEPTH, N_SSD_HEADS)),
        'ssd_norm_g': gain(ks[9], (DEPTH, D_SSD)),
        'fg_bias': 2.0 + 0.5 * jax.random.normal(ks[10], (DEPTH, N_ATT_HEADS), f32),
        'q_norm_g': gain(ks[11], (DEPTH, HEAD_DIM)),
        'k_norm_g': gain(ks[12], (DEPTH, HEAD_DIM)),
        'attn_norm_g': gain(ks[13], (DEPTH, D_ATT)),
        'w_out': nrm(ks[14], (DEPTH, D_MIX, D_MODEL), D_MIX),
        'norm2_g': gain(ks[15], (DEPTH, D_MODEL)),
        'w_gate_dense': nrm(ks[16], (N_DENSE, D_MODEL, D_FF_DENSE), D_MODEL),
        'w_up_dense': nrm(ks[17], (N_DENSE, D_MODEL, D_FF_DENSE), D_MODEL),
        'w_down_dense': nrm(ks[18], (N_DENSE, D_FF_DENSE, D_MODEL), D_FF_DENSE),
        'w_router': nrm(ks[19], (N_MOE, D_MODEL, N_EXPERTS), D_MODEL),
        'w_gate_exp': nrm(ks[20], (N_MOE, N_EXPERTS, D_MODEL, D_FF_EXPERT), D_MODEL),
        'w_up_exp': nrm(ks[21], (N_MOE, N_EXPERTS, D_MODEL, D_FF_EXPERT), D_MODEL),
        'w_down_exp': nrm(ks[22], (N_MOE, N_EXPERTS, D_FF_EXPERT, D_MODEL), D_FF_EXPERT),
        'ple_norm_g': gain(ks[23], (DEPTH, D_MODEL)),
        'w_ple_gate': nrm(ks[24], (DEPTH, D_MODEL, D_MODEL), D_MODEL),
        'w_ple_proj': nrm(ks[25], (DEPTH, PLE_DIM, D_MODEL), PLE_DIM),
    }


def reference(x, p, norm1_g, w_in, conv_w, conv_b, dt_bias, a_log, d_skip, ssd_norm_g,
              fg_bias, q_norm_g, k_norm_g, attn_norm_g, w_out, norm2_g,
              w_gate_dense, w_up_dense, w_down_dense, w_router, w_gate_exp, w_up_exp,
              w_down_exp, ple_norm_g, w_ple_gate, w_ple_proj):
    f32 = jnp.float32
    b, s, _ = x.shape
    split_in = [D_SSD, D_SSD + D_CONV, D_IN_SSD, D_IN_SSD + D_ATT,
                D_IN_SSD + 2 * D_ATT, D_IN_SSD + 3 * D_ATT]
    for i in range(DEPTH):
        u = rmsnorm(x, norm1_g[i])
        proj = u @ w_in[i]
        z, xbc, dt_raw, q, k, v, f_raw = jnp.split(proj, split_in, axis=-1)

        xbc = jax.nn.silu(causal_depthwise_conv(xbc, conv_w[i], conv_b[i]))
        xs, bm, cm = jnp.split(xbc, [D_SSD, D_SSD + SSD_GROUPS * D_STATE], axis=-1)
        dt = jax.nn.softplus(dt_raw.astype(f32) + dt_bias[i].astype(f32))
        a = -jnp.exp(a_log[i].astype(f32))
        xh = xs.reshape(b, s, N_SSD_HEADS, HEAD_DIM).astype(f32)
        y = ssd_chunked(xh, dt, a,
                        bm.reshape(b, s, SSD_GROUPS, D_STATE).astype(f32),
                        cm.reshape(b, s, SSD_GROUPS, D_STATE).astype(f32))
        y = y + d_skip[i].astype(f32)[:, None] * xh
        y_ssd = rmsnorm(y.reshape(b, s, D_SSD) * jax.nn.silu(z.astype(f32)), ssd_norm_g[i]).astype(x.dtype)

        qh = rmsnorm(q.reshape(b, s, N_ATT_HEADS, HEAD_DIM), q_norm_g[i])
        kh = rmsnorm(k.reshape(b, s, N_ATT_HEADS, HEAD_DIM), k_norm_g[i])
        vh = v.reshape(b, s, N_ATT_HEADS, HEAD_DIM)
        log_f = jax.nn.log_sigmoid(f_raw.astype(f32) + fg_bias[i].astype(f32))
        y_att = forgetting_attention(qh, kh, vh, log_f)
        y_att = rmsnorm(y_att.reshape(b, s, D_ATT), attn_norm_g[i])

        x = x + jnp.concatenate([y_ssd, y_att], axis=-1) @ w_out[i]

        u2 = rmsnorm(x, norm2_g[i])
        j = i // 2
        if i % 2 == 0:
            x = x + swiglu(u2, w_gate_dense[j], w_up_dense[j], w_down_dense[j])
        else:
            x = x + moe_swiglu(u2, w_router[j], w_gate_exp[j], w_up_exp[j], w_down_exp[j])

        gate = jax.nn.sigmoid(rmsnorm(x, ple_norm_g[i]) @ w_ple_gate[i])
        x = x + gate * (p[i] @ w_ple_proj[i])
    return x
```

```python
import functools

import jax
import jax.numpy as jnp
from jax import lax
from jax.experimental import pallas as pl
from jax.experimental.pallas import tpu as pltpu

F32 = jnp.float32
BF16 = jnp.bfloat16

EPS = 1e-6
HEAD_DIM = 64
N_HEADS = 8
SSD_GROUPS = 2
D_STATE = 128
CONV_K = 4
CHUNK = 128
TOP_K = 2
LANES = 128
SUBLANES = 8
VMEM_LIMIT = 52 * 1024 * 1024


def _params(*sem):
    return pltpu.CompilerParams(dimension_semantics=sem, vmem_limit_bytes=VMEM_LIMIT)


def _rms(x, g):
    return x * lax.rsqrt(jnp.mean(x * x, axis=-1, keepdims=True) + EPS) * g


def _silu(x):
    return x / (1.0 + jnp.exp(-x))


def _softplus(x):
    return jnp.maximum(x, 0.0) + jnp.log(1.0 + jnp.exp(-jnp.abs(x)))


def _cumsum_rows(x):
    n = x.shape[0]
    row = lax.broadcasted_iota(jnp.int32, x.shape, 0)
    s = 1
    while s < n:
        x = x + jnp.where(row >= s, pltpu.roll(x, s, 0), 0.0)
        s *= 2
    return x


def _expand_heads(v, n_rows):
    lane = lax.broadcasted_iota(jnp.int32, (n_rows, LANES), 1)
    blocks = []
    for j in range(N_HEADS // 2):
        a = jnp.broadcast_to(v[:, 2 * j:2 * j + 1], (n_rows, LANES))
        b = jnp.broadcast_to(v[:, 2 * j + 1:2 * j + 2], (n_rows, LANES))
        blocks.append(jnp.where(lane < HEAD_DIM, a, b))
    return jnp.concatenate(blocks, axis=1)


def _in_proj_kernel(x_ref, g_ref, wm_ref, ws_ref, main_ref, small_ref, *, tn):
    u = _rms(x_ref[...], g_ref[...]).astype(BF16)
    n_main = main_ref.shape[1]
    for n0 in range(0, n_main, tn):
        main_ref[:, n0:n0 + tn] = jnp.dot(
            u, wm_ref[:, n0:n0 + tn], preferred_element_type=F32).astype(BF16)
    small_ref[...] = jnp.dot(u, ws_ref[...], preferred_element_type=F32)


def _in_proj(x2d, g, w_main, w_small, *, tm=512, tn=512):
    t, d = x2d.shape
    n_main = w_main.shape[1]
    return pl.pallas_call(
        functools.partial(_in_proj_kernel, tn=tn),
        out_shape=(jax.ShapeDtypeStruct((t, n_main), BF16),
                   jax.ShapeDtypeStruct((t, LANES), F32)),
        grid=(t // tm,),
        in_specs=[pl.BlockSpec((tm, d), lambda i: (i, 0)),
                  pl.BlockSpec((1, d), lambda i: (0, 0)),
                  pl.BlockSpec((d, n_main), lambda i: (0, 0)),
                  pl.BlockSpec((d, LANES), lambda i: (0, 0))],
        out_specs=(pl.BlockSpec((tm, n_main), lambda i: (i, 0)),
                   pl.BlockSpec((tm, LANES), lambda i: (i, 0))),
        compiler_params=_params("parallel"),
        name="in_proj",
    )(x2d, g, w_main, w_small)


def _ssd_kernel(xbc_ref, z_ref, sm_ref, cw_ref, cb_ref, dtb_ref, alog_ref, dskip_ref, g_ref,
                y_ref, state_ref, tail_ref, *, lb):
    d_ssd = N_HEADS * HEAD_DIM
    gw = d_ssd // SSD_GROUPS

    @pl.when(pl.program_id(1) == 0)
    def _():
        state_ref[...] = jnp.zeros_like(state_ref)
        tail_ref[...] = jnp.zeros_like(tail_ref)

    xbc = xbc_ref[...].astype(F32)
    cat = jnp.concatenate([tail_ref[...], xbc], axis=0)
    tail_ref[...] = xbc[lb - SUBLANES:, :]
    cw = cw_ref[...]
    conv = cb_ref[...] + cw[CONV_K - 1:CONV_K, :] * xbc
    for k in range(CONV_K - 1):
        off = SUBLANES - (CONV_K - 1) + k
        conv = conv + cw[k:k + 1, :] * cat[off:off + lb, :]
    act = _silu(conv)

    a_neg = -jnp.exp(alog_ref[...])
    dtb = dtb_ref[...]
    dskip = dskip_ref[...]
    gain = g_ref[...]
    tri = (lax.broadcasted_iota(jnp.int32, (CHUNK, CHUNK), 0)
           >= lax.broadcasted_iota(jnp.int32, (CHUNK, CHUNK), 1))
    lane = lax.broadcasted_iota(jnp.int32, (CHUNK, LANES), 1)

    for c in range(lb // CHUNK):
        r0 = c * CHUNK
        xs = act[r0:r0 + CHUNK, 0:d_ssd]
        bm = act[r0:r0 + CHUNK, d_ssd:d_ssd + SSD_GROUPS * D_STATE].astype(BF16)
        cm = act[r0:r0 + CHUNK, d_ssd + SSD_GROUPS * D_STATE:].astype(BF16)
        dt = _softplus(sm_ref[r0:r0 + CHUNK, :] + dtb)
        acum = _cumsum_rows(dt * a_neg)
        acum_t = acum.T
        total = acum[CHUNK - 1:CHUNK, :]

        dt_e = _expand_heads(dt, CHUNK)
        in_decay_e = _expand_heads(jnp.exp(acum), CHUNK)
        out_decay_e = _expand_heads(jnp.exp(total - acum), CHUNK)
        chunk_decay_e = _expand_heads(jnp.exp(total), 1)

        xdt = xs * dt_e
        xdt_b = xdt.astype(BF16)
        w_state = (xdt * out_decay_e).astype(BF16)

        y_parts = []
        for g in range(SSD_GROUPS):
            bg = bm[:, g * D_STATE:(g + 1) * D_STATE]
            cg = cm[:, g * D_STATE:(g + 1) * D_STATE]
            cb = lax.dot_general(cg, bg, (((1,), (1,)), ((), ())),
                                 preferred_element_type=F32)
            st = state_ref[g]
            y_off = jnp.dot(cg, st.astype(BF16), preferred_element_type=F32)
            for pr in range(gw // LANES):
                pair = []
                col0 = g * gw + pr * LANES
                for hh in range(2):
                    h = (col0 // HEAD_DIM) + hh
                    seg = acum[:, h:h + 1] - acum_t[h:h + 1, :]
                    m = (cb * jnp.where(tri, jnp.exp(seg), 0.0)).astype(BF16)
                    pair.append(jnp.dot(m, xdt_b[:, col0:col0 + LANES],
                                        preferred_element_type=F32))
                y_parts.append(jnp.where(lane < HEAD_DIM, pair[0], pair[1])
                               + y_off[:, pr * LANES:(pr + 1) * LANES]
                               * in_decay_e[:, col0:col0 + LANES])
            upd = lax.dot_general(bg, w_state[:, g * gw:(g + 1) * gw], (((0,), (0,)), ((), ())),
                                  preferred_element_type=F32)
            state_ref[g] = st * chunk_decay_e[:, g * gw:(g + 1) * gw] + upd

        y = jnp.concatenate(y_parts, axis=1) + dskip * xs
        y = y * _silu(z_ref[r0:r0 + CHUNK, :].astype(F32))
        y_ref[r0:r0 + CHUNK, :] = _rms(y, gain).astype(BF16)


def _ssd(main, small, conv_w, conv_b, dtb, alog, dskip_e, gain, *, batch, seq, lb=512):
    t = batch * seq
    nj = seq // lb
    d_ssd = N_HEADS * HEAD_DIM
    d_conv = conv_w.shape[1]
    z_blk = d_conv // d_ssd
    vec = lambda n: pl.BlockSpec((1, n), lambda b, j: (0, 0))
    return pl.pallas_call(
        functools.partial(_ssd_kernel, lb=lb),
        out_shape=jax.ShapeDtypeStruct((t, d_ssd), BF16),
        grid=(batch, nj),
        in_specs=[pl.BlockSpec((lb, d_conv), lambda b, j: (b * nj + j, 0)),
                  pl.BlockSpec((lb, d_ssd), lambda b, j: (b * nj + j, z_blk)),
                  pl.BlockSpec((lb, LANES), lambda b, j: (b * nj + j, 0)),
                  pl.BlockSpec((CONV_K, d_conv), lambda b, j: (0, 0)),
                  vec(d_conv), vec(LANES), vec(LANES), vec(d_ssd), vec(d_ssd)],
        out_specs=pl.BlockSpec((lb, d_ssd), lambda b, j: (b * nj + j, 0)),
        scratch_shapes=[pltpu.VMEM((SSD_GROUPS, D_STATE, d_ssd // SSD_GROUPS), F32),
                        pltpu.VMEM((SUBLANES, d_conv), F32)],
        compiler_params=_params("parallel", "arbitrary"),
        name="ssd",
    )(main, main, small, conv_w, conv_b, dtb, alog, dskip_e, gain)


def _attn_prep_kernel(q_ref, k_ref, sm_ref, fgb_ref, gq_ref, gk_ref, ones_ref,
                      qa_ref, ka_ref, carry_ref, *, ts):
    @pl.when(pl.program_id(1) == 0)
    def _():
        carry_ref[...] = jnp.zeros_like(carry_ref)

    def headnorm(ref, g):
        v = ref[...].astype(F32)
        ssq = jnp.dot((v * v).astype(BF16), ones_ref[...], preferred_element_type=F32)
        return v * lax.rsqrt(ssq * (1.0 / HEAD_DIM) + EPS) * g

    qn = headnorm(q_ref, gq_ref[...])
    kn = headnorm(k_ref, gk_ref[...])

    xf = sm_ref[...] + fgb_ref[...]
    log_f = jnp.minimum(xf, 0.0) - jnp.log(1.0 + jnp.exp(-jnp.abs(xf)))
    fcum = _cumsum_rows(log_f) + carry_ref[...]
    carry_ref[...] = fcum[ts - 1:ts, :]
    f_hi = fcum.astype(BF16).astype(F32)
    rem = fcum - f_hi
    f_mid = rem.astype(BF16).astype(F32)
    f_lo = rem - f_mid

    lane = lax.broadcasted_iota(jnp.int32, (ts, LANES), 1)
    for h in range(N_HEADS):
        j = h // 2
        qb = qn[:, j * LANES:(j + 1) * LANES]
        kb = kn[:, j * LANES:(j + 1) * LANES]
        if h % 2:
            qb = pltpu.roll(qb, HEAD_DIM, 1)
            kb = pltpu.roll(kb, HEAD_DIM, 1)
        fl = N_HEADS + h
        bh = jnp.broadcast_to(f_hi[:, fl:fl + 1], (ts, LANES))
        bm = jnp.broadcast_to(f_mid[:, fl:fl + 1], (ts, LANES))
        bl = jnp.broadcast_to(f_lo[:, fl:fl + 1], (ts, LANES))
        d = HEAD_DIM
        qx = jnp.where(lane == d, bh, jnp.where(lane == d + 1, bm, jnp.where(
            lane == d + 2, bl, jnp.where(lane < d + 6, 1.0, 0.0))))
        kx = jnp.where(lane < d + 3, 1.0, jnp.where(lane == d + 3, -bh, jnp.where(
            lane == d + 4, -bm, jnp.where(lane == d + 5, -bl, 0.0))))
        qa_ref[0, h] = jnp.where(lane < d, qb, qx).astype(BF16)
        ka_ref[0, h] = jnp.where(lane < d, kb, kx).astype(BF16)


def _attn_prep(main, small, fgb, gq, gk, ones_bd, *, batch, seq, q_blk, k_blk, ts=512):
    nj = seq // ts
    d_att = N_HEADS * HEAD_DIM
    vec = lambda n: pl.BlockSpec((1, n), lambda b, j: (0, 0))
    aug = jax.ShapeDtypeStruct((batch, N_HEADS, seq, LANES), BF16)
    aug_spec = pl.BlockSpec((1, N_HEADS, ts, LANES), lambda b, j: (b, 0, j, 0))
    return pl.pallas_call(
        functools.partial(_attn_prep_kernel, ts=ts),
        out_shape=(aug, aug),
        grid=(batch, nj),
        in_specs=[pl.BlockSpec((ts, d_att), lambda b, j: (b * nj + j, q_blk)),
                  pl.BlockSpec((ts, d_att), lambda b, j: (b * nj + j, k_blk)),
                  pl.BlockSpec((ts, LANES), lambda b, j: (b * nj + j, 0)),
                  vec(LANES), vec(d_att), vec(d_att),
                  pl.BlockSpec((d_att, d_att), lambda b, j: (0, 0))],
        out_specs=(aug_spec, aug_spec),
        scratch_shapes=[pltpu.VMEM((1, LANES), F32)],
        compiler_params=_params("parallel", "arbitrary"),
        name="attn_prep",
    )(main, main, small, fgb, gq, gk, ones_bd)


def _flash_kernel(qa_ref, ka_ref, v_ref, o_ref, *, bq):
    i = pl.program_id(2)
    lane = lax.broadcasted_iota(jnp.int32, (bq, LANES), 1)
    tri = (lax.broadcasted_iota(jnp.int32, (bq, bq), 0)
           >= lax.broadcasted_iota(jnp.int32, (bq, bq), 1))
    outs = []
    for hh in range(2):
        q = qa_ref[0, hh]

        def block(kb, carry, masked):
            m, l, acc = carry
            k0 = pl.multiple_of(kb * bq, bq)
            k = ka_ref[0, hh, pl.ds(k0, bq), :]
            v = v_ref[pl.ds(k0, bq), :]
            s = lax.dot_general(q, k, (((1,), (1,)), ((), ())), preferred_element_type=F32)
            if masked:
                s = jnp.where(tri, s, -jnp.inf)
            m_new = jnp.maximum(m, jnp.max(s, axis=-1, keepdims=True))
            alpha = jnp.exp(m - m_new)
            p = jnp.exp(s - m_new)
            l = alpha * l + jnp.sum(p, axis=-1, keepdims=True)
            acc = alpha * acc + jnp.dot(p.astype(BF16), v, preferred_element_type=F32)
            return m_new, l, acc

        init = (jnp.full((bq, 1), -jnp.inf, F32), jnp.zeros((bq, 1), F32),
                jnp.zeros((bq, LANES), F32))
        carry = lax.fori_loop(0, i, lambda kb, c: block(kb, c, False), init)
        m, l, acc = block(i, carry, True)
        outs.append(acc / l)
    o_ref[...] = jnp.where(lane < HEAD_DIM, outs[0], outs[1]).astype(BF16)


def _flash(qa, ka, main, *, batch, seq, v_blk, bq=512):
    nq = seq // bq
    npair = N_HEADS // 2
    return pl.pallas_call(
        functools.partial(_flash_kernel, bq=bq),
        out_shape=jax.ShapeDtypeStruct((batch * seq, N_HEADS * HEAD_DIM), BF16),
        grid=(batch, npair, nq),
        in_specs=[pl.BlockSpec((1, 2, bq, LANES), lambda b, hp, i: (b, hp, i, 0)),
                  pl.BlockSpec((1, 2, seq, LANES), lambda b, hp, i: (b, hp, 0, 0)),
                  pl.BlockSpec((seq, LANES), lambda b, hp, i: (b, v_blk + hp))],
        out_specs=pl.BlockSpec((bq, LANES), lambda b, hp, i: (b * nq + i, hp)),
        compiler_params=_params("parallel", "parallel", "arbitrary"),
        name="flash",
    )(qa, ka, main)


def _out_proj_kernel(x_ref, ys_ref, ya_ref, ga_ref, wt_ref, wb_ref, o_ref):
    ya = _rms(ya_ref[...].astype(F32), ga_ref[...]).astype(BF16)
    o_ref[...] = (x_ref[...]
                  + jnp.dot(ys_ref[...], wt_ref[...], preferred_element_type=F32)
                  + jnp.dot(ya, wb_ref[...], preferred_element_type=F32))


def _out_proj(x2d, y_ssd, y_att, ga, w_top, w_bot, *, tm=512):
    t, d = x2d.shape
    dh = y_ssd.shape[1]
    row = lambda n: pl.BlockSpec((tm, n), lambda i: (i, 0))
    full = lambda a: pl.BlockSpec(a.shape, lambda i: (0, 0))
    return pl.pallas_call(
        _out_proj_kernel,
        out_shape=jax.ShapeDtypeStruct((t, d), F32),
        grid=(t // tm,),
        in_specs=[row(d), row(dh), row(dh), full(ga), full(w_top), full(w_bot)],
        out_specs=row(d),
        compiler_params=_params("parallel"),
        name="out_proj",
    )(x2d, y_ssd, y_att, ga, w_top, w_bot)


def _ffn_chunks(d_ff, step):
    return [(f0, min(step, d_ff - f0)) for f0 in range(0, d_ff, step)]


def _ffn_dense_kernel(x_ref, g_ref, wg_ref, wu_ref, wd_ref, o_ref, *, fc):
    x = x_ref[...]
    u = _rms(x, g_ref[...]).astype(BF16)
    acc = x
    for f0, fw in _ffn_chunks(wg_ref.shape[1], fc):
        hg = jnp.dot(u, wg_ref[:, f0:f0 + fw], preferred_element_type=F32)
        hu = jnp.dot(u, wu_ref[:, f0:f0 + fw], preferred_element_type=F32)
        h = (_silu(hg) * hu).astype(BF16)
        acc = acc + jnp.dot(h, wd_ref[f0:f0 + fw, :], preferred_element_type=F32)
    o_ref[...] = acc


def _ffn_dense(x2d, g, wg, wu, wd, *, tm=512, fc=1024):
    t, d = x2d.shape
    row = pl.BlockSpec((tm, d), lambda i: (i, 0))
    full = lambda a: pl.BlockSpec(a.shape, lambda i: (0, 0))
    return pl.pallas_call(
        functools.partial(_ffn_dense_kernel, fc=fc),
        out_shape=jax.ShapeDtypeStruct((t, d), F32),
        grid=(t // tm,),
        in_specs=[row, full(g), full(wg), full(wu), full(wd)],
        out_specs=row,
        compiler_params=_params("parallel"),
        name="ffn_dense",
    )(x2d, g, wg, wu, wd)


def _moe_kernel(x_ref, g_ref, wrh_ref, wrl_ref, wg_ref, wu_ref, wd_ref, o_ref,
                u_ref, comb_ref, *, n_experts, fc):
    e = pl.program_id(1)
    tm = x_ref.shape[0]
    lane = lax.broadcasted_iota(jnp.int32, (tm, LANES), 1)

    @pl.when(e == 0)
    def _():
        x = x_ref[...]
        u = _rms(x, g_ref[...])
        u_hi = u.astype(BF16)
        u_lo = (u - u_hi.astype(F32)).astype(BF16)
        u_ref[...] = u_hi
        logits = (jnp.dot(u_hi, wrh_ref[...], preferred_element_type=F32)
                  + jnp.dot(u_lo, wrh_ref[...], preferred_element_type=F32)
                  + jnp.dot(u_hi, wrl_ref[...], preferred_element_type=F32))
        logits = jnp.where(lane < n_experts, logits, -jnp.inf)
        m1 = jnp.max(logits, axis=-1, keepdims=True)
        i1 = jnp.min(jnp.where(logits == m1, lane, LANES), axis=-1, keepdims=True)
        rest = jnp.where(lane == i1, -jnp.inf, logits)
        m2 = jnp.max(rest, axis=-1, keepdims=True)
        i2 = jnp.min(jnp.where(rest == m2, lane, LANES), axis=-1, keepdims=True)
        e2 = jnp.exp(m2 - m1)
        g1 = 1.0 / (1.0 + e2)
        comb_ref[...] = jnp.where(lane == i1, g1, 0.0) + jnp.where(lane == i2, e2 * g1, 0.0)
        o_ref[...] = x

    u = u_ref[...]
    ce = jnp.sum(jnp.where(lane == e, comb_ref[...], 0.0), axis=-1, keepdims=True)
    y = jnp.zeros(o_ref.shape, F32)
    for f0, fw in _ffn_chunks(wg_ref.shape[2], fc):
        hg = jnp.dot(u, wg_ref[0, :, f0:f0 + fw], preferred_element_type=F32)
        hu = jnp.dot(u, wu_ref[0, :, f0:f0 + fw], preferred_element_type=F32)
        h = (_silu(hg) * hu).astype(BF16)
        y = y + jnp.dot(h, wd_ref[0, f0:f0 + fw, :], preferred_element_type=F32)
    o_ref[...] += ce * y


def _moe(x2d, g, wr_hi, wr_lo, wg, wu, wd, *, n_experts, tm=1024, fc=768):
    t, d = x2d.shape
    dff = wg.shape[2]
    row = pl.BlockSpec((tm, d), lambda i, e: (i, 0))
    full = lambda a: pl.BlockSpec(a.shape, lambda i, e: (0, 0))
    return pl.pallas_call(
        functools.partial(_moe_kernel, n_experts=n_experts, fc=fc),
        out_shape=jax.ShapeDtypeStruct((t, d), F32),
        grid=(t // tm, n_experts),
        in_specs=[row, full(g), full(wr_hi), full(wr_lo),
                  pl.BlockSpec((1, d, dff), lambda i, e: (e, 0, 0)),
                  pl.BlockSpec((1, d, dff), lambda i, e: (e, 0, 0)),
                  pl.BlockSpec((1, dff, d), lambda i, e: (e, 0, 0))],
        out_specs=row,
        scratch_shapes=[pltpu.VMEM((tm, d), BF16), pltpu.VMEM((tm, LANES), F32)],
        compiler_params=_params("parallel", "arbitrary"),
        name="moe",
    )(x2d, g, wr_hi, wr_lo, wg, wu, wd)


def _ple_kernel(x_ref, p_ref, g_ref, wg_ref, wp_ref, o_ref):
    x = x_ref[...]
    u = _rms(x, g_ref[...]).astype(BF16)
    gate = 1.0 / (1.0 + jnp.exp(-jnp.dot(u, wg_ref[...], preferred_element_type=F32)))
    pe = jnp.dot(p_ref[...].astype(BF16), wp_ref[...], preferred_element_type=F32)
    o_ref[...] = x + gate * pe


def _ple(x2d, p2d, g, wg, wp, *, tm=512):
    t, d = x2d.shape
    dp = p2d.shape[1]
    full = lambda a: pl.BlockSpec(a.shape, lambda i: (0, 0))
    return pl.pallas_call(
        _ple_kernel,
        out_shape=jax.ShapeDtypeStruct((t, d), F32),
        grid=(t // tm,),
        in_specs=[pl.BlockSpec((tm, d), lambda i: (i, 0)),
                  pl.BlockSpec((tm, dp), lambda i: (i, 0)),
                  full(g), full(wg), full(wp)],
        out_specs=pl.BlockSpec((tm, d), lambda i: (i, 0)),
        compiler_params=_params("parallel"),
        name="ple",
    )(x2d, p2d, g, wg, wp)


def _pad_lanes(v, offset=0):
    out = jnp.zeros((1, LANES), F32)
    return out.at[0, offset:offset + v.shape[0]].set(v.astype(F32))


def kernel(x, p, norm1_g, w_in, conv_w, conv_b, dt_bias, a_log, d_skip, ssd_norm_g, fg_bias, q_norm_g, k_norm_g, attn_norm_g, w_out, norm2_g, w_gate_dense, w_up_dense, w_down_dense, w_router, w_gate_exp, w_up_exp, w_down_exp, ple_norm_g, w_ple_gate, w_ple_proj):
    batch, seq, d_model = x.shape
    depth = w_in.shape[0]
    t = batch * seq
    d_ssd = N_HEADS * HEAD_DIM
    d_att = N_HEADS * HEAD_DIM
    d_conv = conv_w.shape[2]
    n_experts = w_router.shape[2]
    o_xbc = d_ssd
    o_dt = o_xbc + d_conv
    o_q = o_dt + N_HEADS
    o_k = o_q + d_att
    o_v = o_k + d_att
    o_f = o_v + d_att
    q_blk = (d_conv + d_ssd) // d_att
    k_blk = q_blk + 1
    v_blk = (d_conv + d_ssd + 2 * d_att) // LANES

    ones_bd = (jnp.arange(d_att)[:, None] // HEAD_DIM
               == jnp.arange(d_att)[None, :] // HEAD_DIM).astype(BF16)
    row = lambda v: v.astype(F32).reshape(1, -1)

    xf = x.reshape(t, d_model)
    for i in range(depth):
        w = w_in[i]
        w_main = jnp.concatenate(
            [w[:, o_xbc:o_dt], w[:, :d_ssd], w[:, o_q:o_k], w[:, o_k:o_v], w[:, o_v:o_f]],
            axis=1).astype(BF16)
        w_small = jnp.zeros((d_model, LANES), F32)
        w_small = w_small.at[:, :N_HEADS].set(w[:, o_dt:o_q])
        w_small = w_small.at[:, N_HEADS:2 * N_HEADS].set(w[:, o_f:o_f + N_HEADS]).astype(BF16)

        main, small = _in_proj(xf, row(norm1_g[i]), w_main, w_small)

        y_ssd = _ssd(main, small, conv_w[i].astype(F32), row(conv_b[i]),
                     _pad_lanes(dt_bias[i]), _pad_lanes(a_log[i]),
                     row(jnp.repeat(d_skip[i], HEAD_DIM)), row(ssd_norm_g[i]),
                     batch=batch, seq=seq)

        gq = row(jnp.tile(q_norm_g[i], N_HEADS)) * (HEAD_DIM ** -0.5)
        gk = row(jnp.tile(k_norm_g[i], N_HEADS))
        qa, ka = _attn_prep(main, small, _pad_lanes(fg_bias[i], N_HEADS), gq, gk, ones_bd,
                            batch=batch, seq=seq, q_blk=q_blk, k_blk=k_blk)
        y_att = _flash(qa, ka, main, batch=batch, seq=seq, v_blk=v_blk)

        wo = w_out[i].astype(BF16)
        xf = _out_proj(xf, y_ssd, y_att, row(attn_norm_g[i]), wo[:d_ssd], wo[d_ssd:])

        j = i // 2
        if i % 2 == 0:
            xf = _ffn_dense(xf, row(norm2_g[i]), w_gate_dense[j].astype(BF16),
                            w_up_dense[j].astype(BF16), w_down_dense[j].astype(BF16))
        else:
            wr = jnp.zeros((d_model, LANES), F32).at[:, :n_experts].set(w_router[j])
            wr_hi = wr.astype(BF16)
            wr_lo = (wr - wr_hi.astype(F32)).astype(BF16)
            xf = _moe(xf, row(norm2_g[i]), wr_hi, wr_lo, w_gate_exp[j].astype(BF16),
                      w_up_exp[j].astype(BF16), w_down_exp[j].astype(BF16), n_experts=n_experts)

        xf = _ple(xf, p[i].reshape(t, -1), row(ple_norm_g[i]), w_ple_gate[i].astype(BF16),
                  w_ple_proj[i].astype(BF16))
    return xf.reshape(batch, seq, d_model)
```

```python
import functools

import jax
import jax.numpy as jnp
from jax import lax
from jax.experimental import pallas as pl
from jax.experimental.pallas import tpu as pltpu

F32 = jnp.float32
BF16 = jnp.bfloat16

EPS = 1e-6
LOG2E = 1.4426950408889634
HEAD_DIM = 64
N_HEADS = 8
SSD_GROUPS = 2
D_STATE = 128
CONV_K = 4
CHUNK = 128
TOP_K = 2
LANES = 128
SUBLANES = 8
VMEM_LIMIT = 52 * 1024 * 1024


def _params(*sem):
    return pltpu.CompilerParams(dimension_semantics=sem, vmem_limit_bytes=VMEM_LIMIT)


def _rms(x, g):
    return x * lax.rsqrt(jnp.mean(x * x, axis=-1, keepdims=True) + EPS) * g


def _silu(x):
    return x / (1.0 + jnp.exp(-x))


def _softplus(x):
    return jnp.maximum(x, 0.0) + jnp.log(1.0 + jnp.exp(-jnp.abs(x)))


def _cumsum_rows(x):
    n = x.shape[0]
    row = lax.broadcasted_iota(jnp.int32, x.shape, 0)
    s = 1
    while s < n:
        x = x + jnp.where(row >= s, pltpu.roll(x, s, 0), 0.0)
        s *= 2
    return x


def _expand_heads(v, n_rows):
    lane = lax.broadcasted_iota(jnp.int32, (n_rows, LANES), 1)
    blocks = []
    for j in range(N_HEADS // 2):
        a = jnp.broadcast_to(v[:, 2 * j:2 * j + 1], (n_rows, LANES))
        b = jnp.broadcast_to(v[:, 2 * j + 1:2 * j + 2], (n_rows, LANES))
        blocks.append(jnp.where(lane < HEAD_DIM, a, b))
    return jnp.concatenate(blocks, axis=1)


def _in_proj_kernel(x_ref, g_ref, wm_ref, ws_ref, main_ref, small_ref, *, tn):
    u = _rms(x_ref[...], g_ref[...]).astype(BF16)
    n_main = main_ref.shape[1]
    for n0 in range(0, n_main, tn):
        main_ref[:, n0:n0 + tn] = jnp.dot(
            u, wm_ref[:, n0:n0 + tn], preferred_element_type=F32).astype(BF16)
    small_ref[...] = jnp.dot(u, ws_ref[...], preferred_element_type=F32)


def _in_proj(x2d, g, w_main, w_small, *, tm=512, tn=512):
    t, d = x2d.shape
    n_main = w_main.shape[1]
    return pl.pallas_call(
        functools.partial(_in_proj_kernel, tn=tn),
        out_shape=(jax.ShapeDtypeStruct((t, n_main), BF16),
                   jax.ShapeDtypeStruct((t, LANES), F32)),
        grid=(t // tm,),
        in_specs=[pl.BlockSpec((tm, d), lambda i: (i, 0)),
                  pl.BlockSpec((1, d), lambda i: (0, 0)),
                  pl.BlockSpec((d, n_main), lambda i: (0, 0)),
                  pl.BlockSpec((d, LANES), lambda i: (0, 0))],
        out_specs=(pl.BlockSpec((tm, n_main), lambda i: (i, 0)),
                   pl.BlockSpec((tm, LANES), lambda i: (i, 0))),
        compiler_params=_params("parallel"),
        name="in_proj",
    )(x2d, g, w_main, w_small)


def _ssd_kernel(xbc_ref, z_ref, sm_ref, cw_ref, cb_ref, dtb_ref, alog_ref, dskip_ref, g_ref,
                y_ref, state_ref, tail_ref, *, lb):
    d_ssd = N_HEADS * HEAD_DIM
    gw = d_ssd // SSD_GROUPS

    @pl.when(pl.program_id(1) == 0)
    def _():
        state_ref[...] = jnp.zeros_like(state_ref)
        tail_ref[...] = jnp.zeros_like(tail_ref)

    xbc = xbc_ref[...].astype(F32)
    cat = jnp.concatenate([tail_ref[...], xbc], axis=0)
    tail_ref[...] = xbc[lb - SUBLANES:, :]
    cw = cw_ref[...]
    conv = cb_ref[...] + cw[CONV_K - 1:CONV_K, :] * xbc
    for k in range(CONV_K - 1):
        off = SUBLANES - (CONV_K - 1) + k
        conv = conv + cw[k:k + 1, :] * cat[off:off + lb, :]
    act = _silu(conv)

    a_neg = -jnp.exp(alog_ref[...])
    dtb = dtb_ref[...]
    dskip = dskip_ref[...]
    gain = g_ref[...]
    tri = (lax.broadcasted_iota(jnp.int32, (CHUNK, CHUNK), 0)
           >= lax.broadcasted_iota(jnp.int32, (CHUNK, CHUNK), 1))
    lane = lax.broadcasted_iota(jnp.int32, (CHUNK, LANES), 1)

    for c in range(lb // CHUNK):
        r0 = c * CHUNK
        xs = act[r0:r0 + CHUNK, 0:d_ssd]
        bm = act[r0:r0 + CHUNK, d_ssd:d_ssd + SSD_GROUPS * D_STATE].astype(BF16)
        cm = act[r0:r0 + CHUNK, d_ssd + SSD_GROUPS * D_STATE:].astype(BF16)
        dt = _softplus(sm_ref[r0:r0 + CHUNK, :] + dtb)
        acum = _cumsum_rows(dt * a_neg)
        acum_t = acum.T
        total = acum[CHUNK - 1:CHUNK, :]

        dt_e = _expand_heads(dt, CHUNK)
        in_decay_e = _expand_heads(jnp.exp(acum), CHUNK)
        out_decay_e = _expand_heads(jnp.exp(total - acum), CHUNK)
        chunk_decay_e = _expand_heads(jnp.exp(total), 1)

        xdt = xs * dt_e
        xdt_b = xdt.astype(BF16)
        w_state = (xdt * out_decay_e).astype(BF16)

        y_parts = []
        for g in range(SSD_GROUPS):
            bg = bm[:, g * D_STATE:(g + 1) * D_STATE]
            cg = cm[:, g * D_STATE:(g + 1) * D_STATE]
            cb = lax.dot_general(cg, bg, (((1,), (1,)), ((), ())),
                                 preferred_element_type=F32)
            st = state_ref[g]
            y_off = jnp.dot(cg, st.astype(BF16), preferred_element_type=F32)
            for pr in range(gw // LANES):
                pair = []
                col0 = g * gw + pr * LANES
                for hh in range(2):
                    h = (col0 // HEAD_DIM) + hh
                    seg = acum[:, h:h + 1] - acum_t[h:h + 1, :]
                    m = (cb * jnp.where(tri, jnp.exp(seg), 0.0)).astype(BF16)
                    pair.append(jnp.dot(m, xdt_b[:, col0:col0 + LANES],
                                        preferred_element_type=F32))
                y_parts.append(jnp.where(lane < HEAD_DIM, pair[0], pair[1])
                               + y_off[:, pr * LANES:(pr + 1) * LANES]
                               * in_decay_e[:, col0:col0 + LANES])
            upd = lax.dot_general(bg, w_state[:, g * gw:(g + 1) * gw], (((0,), (0,)), ((), ())),
                                  preferred_element_type=F32)
            state_ref[g] = st * chunk_decay_e[:, g * gw:(g + 1) * gw] + upd

        y = jnp.concatenate(y_parts, axis=1) + dskip * xs
        y = y * _silu(z_ref[r0:r0 + CHUNK, :].astype(F32))
        y_ref[r0:r0 + CHUNK, :] = _rms(y, gain).astype(BF16)


def _ssd(main, small, conv_w, conv_b, dtb, alog, dskip_e, gain, *, batch, seq, lb=512):
    t = batch * seq
    nj = seq // lb
    d_ssd = N_HEADS * HEAD_DIM
    d_conv = conv_w.shape[1]
    z_blk = d_conv // d_ssd
    vec = lambda n: pl.BlockSpec((1, n), lambda b, j: (0, 0))
    return pl.pallas_call(
        functools.partial(_ssd_kernel, lb=lb),
        out_shape=jax.ShapeDtypeStruct((t, d_ssd), BF16),
        grid=(batch, nj),
        in_specs=[pl.BlockSpec((lb, d_conv), lambda b, j: (b * nj + j, 0)),
                  pl.BlockSpec((lb, d_ssd), lambda b, j: (b * nj + j, z_blk)),
                  pl.BlockSpec((lb, LANES), lambda b, j: (b * nj + j, 0)),
                  pl.BlockSpec((CONV_K, d_conv), lambda b, j: (0, 0)),
                  vec(d_conv), vec(LANES), vec(LANES), vec(d_ssd), vec(d_ssd)],
        out_specs=pl.BlockSpec((lb, d_ssd), lambda b, j: (b * nj + j, 0)),
        scratch_shapes=[pltpu.VMEM((SSD_GROUPS, D_STATE, d_ssd // SSD_GROUPS), F32),
                        pltpu.VMEM((SUBLANES, d_conv), F32)],
        compiler_params=_params("parallel", "arbitrary"),
        name="ssd",
    )(main, main, small, conv_w, conv_b, dtb, alog, dskip_e, gain)


def _attn_prep_kernel(q_ref, k_ref, v_ref, sm_ref, fgb_ref, gq_ref, gk_ref, ones_ref,
                      qa_ref, ka_ref, va_ref, carry_ref, *, ts):
    @pl.when(pl.program_id(1) == 0)
    def _():
        carry_ref[...] = jnp.zeros_like(carry_ref)

    def headnorm(ref, g):
        v = ref[...].astype(F32)
        ssq = jnp.dot((v * v).astype(BF16), ones_ref[...], preferred_element_type=F32)
        return v * lax.rsqrt(ssq * (1.0 / HEAD_DIM) + EPS) * g

    qn = headnorm(q_ref, gq_ref[...])
    kn = headnorm(k_ref, gk_ref[...])
    vv = v_ref[...].astype(F32)

    xf = sm_ref[...] + fgb_ref[...]
    log_f = jnp.minimum(xf, 0.0) - jnp.log(1.0 + jnp.exp(-jnp.abs(xf)))
    fcum = _cumsum_rows(log_f) + carry_ref[...]
    carry_ref[...] = fcum[ts - 1:ts, :]
    f2 = fcum * LOG2E
    f_hi = f2.astype(BF16).astype(F32)
    rem = f2 - f_hi
    f_mid = rem.astype(BF16).astype(F32)
    f_lo = rem - f_mid

    lane = lax.broadcasted_iota(jnp.int32, (ts, LANES), 1)
    d = HEAD_DIM
    v_tail = jnp.where(lane == d, 1.0, 0.0)
    for h in range(N_HEADS):
        j = h // 2
        qb = qn[:, j * LANES:(j + 1) * LANES]
        kb = kn[:, j * LANES:(j + 1) * LANES]
        vb = vv[:, j * LANES:(j + 1) * LANES]
        if h % 2:
            qb = pltpu.roll(qb, HEAD_DIM, 1)
            kb = pltpu.roll(kb, HEAD_DIM, 1)
            vb = pltpu.roll(vb, HEAD_DIM, 1)
        va_ref[0, h] = jnp.where(lane < d, vb, v_tail).astype(BF16)
        fl = N_HEADS + h
        bh = jnp.broadcast_to(f_hi[:, fl:fl + 1], (ts, LANES))
        bm = jnp.broadcast_to(f_mid[:, fl:fl + 1], (ts, LANES))
        bl = jnp.broadcast_to(f_lo[:, fl:fl + 1], (ts, LANES))
        qx = jnp.where(lane == d, bh, jnp.where(lane == d + 1, bm, jnp.where(
            lane == d + 2, bl, jnp.where(lane < d + 6, 1.0, 0.0))))
        kx = jnp.where(lane < d + 3, 1.0, jnp.where(lane == d + 3, -bh, jnp.where(
            lane == d + 4, -bm, jnp.where(lane == d + 5, -bl, 0.0))))
        qa_ref[0, h] = jnp.where(lane < d, qb, qx).astype(BF16)
        ka_ref[0, h] = jnp.where(lane < d, kb, kx).astype(BF16)


def _attn_prep(main, small, fgb, gq, gk, ones_bd, *, batch, seq, q_blk, ts=512):
    nj = seq // ts
    d_att = N_HEADS * HEAD_DIM
    vec = lambda n: pl.BlockSpec((1, n), lambda b, j: (0, 0))
    aug = jax.ShapeDtypeStruct((batch, N_HEADS, seq, LANES), BF16)
    aug_spec = pl.BlockSpec((1, N_HEADS, ts, LANES), lambda b, j: (b, 0, j, 0))
    col = lambda c: pl.BlockSpec((ts, d_att), lambda b, j: (b * nj + j, c))
    return pl.pallas_call(
        functools.partial(_attn_prep_kernel, ts=ts),
        out_shape=(aug, aug, aug),
        grid=(batch, nj),
        in_specs=[col(q_blk), col(q_blk + 1), col(q_blk + 2),
                  pl.BlockSpec((ts, LANES), lambda b, j: (b * nj + j, 0)),
                  vec(LANES), vec(d_att), vec(d_att),
                  pl.BlockSpec((d_att, d_att), lambda b, j: (0, 0))],
        out_specs=(aug_spec, aug_spec, aug_spec),
        scratch_shapes=[pltpu.VMEM((1, LANES), F32)],
        compiler_params=_params("parallel", "arbitrary"),
        name="attn_prep",
    )(main, main, main, small, fgb, gq, gk, ones_bd)


def _flash_kernel(qa_ref, ka_ref, va_ref, o_ref, s_ref, m_ref, acc_ref, *, bq):
    i = pl.program_id(2)
    lane = lax.broadcasted_iota(jnp.int32, (bq, LANES), 1)
    tri = (lax.broadcasted_iota(jnp.int32, (bq, bq), 0)
           >= lax.broadcasted_iota(jnp.int32, (bq, bq), 1))

    def logits(kb, slot):
        k0 = pl.multiple_of(kb * bq, bq)
        for hh in range(2):
            s_ref[slot, hh] = lax.dot_general(
                qa_ref[0, hh], ka_ref[0, hh, pl.ds(k0, bq), :], (((1,), (1,)), ((), ())),
                preferred_element_type=F32)

    def consume(kb, slot, masked):
        k0 = pl.multiple_of(kb * bq, bq)
        for hh in range(2):
            s = s_ref[slot, hh]
            if masked:
                s = jnp.where(tri, s, -jnp.inf)
            m = m_ref[hh]
            m_new = jnp.maximum(m, jnp.max(s, axis=-1, keepdims=True))
            p = jnp.exp2((s - m_new).astype(BF16))
            acc_ref[hh] = (jnp.exp2(m - m_new) * acc_ref[hh]
                           + jnp.dot(p, va_ref[0, hh, pl.ds(k0, bq), :],
                                     preferred_element_type=F32))
            m_ref[hh] = m_new

    m_ref[...] = jnp.full(m_ref.shape, -jnp.inf, F32)
    acc_ref[...] = jnp.zeros(acc_ref.shape, F32)
    logits(0, 0)

    def pair(j, carry):
        kb = 2 * j
        logits(kb + 1, 1)
        consume(kb, 0, False)
        logits(kb + 2, 0)
        consume(kb + 1, 1, False)
        return carry

    lax.fori_loop(0, i // 2, pair, 0)

    @pl.when(i % 2 == 0)
    def _():
        consume(i, 0, True)

    @pl.when(i % 2 == 1)
    def _():
        logits(i, 1)
        consume(i - 1, 0, False)
        consume(i, 1, True)

    acc0 = acc_ref[0]
    acc1 = acc_ref[1]
    o0 = acc0 / acc0[:, HEAD_DIM:HEAD_DIM + 1]
    o1 = acc1 / acc1[:, HEAD_DIM:HEAD_DIM + 1]
    o_ref[...] = jnp.where(lane < HEAD_DIM, o0, pltpu.roll(o1, HEAD_DIM, 1)).astype(BF16)


def _flash(qa, ka, va, *, batch, seq, bq=512):
    nq = seq // bq
    npair = N_HEADS // 2
    whole = pl.BlockSpec((1, 2, seq, LANES), lambda b, hp, i: (b, hp, 0, 0))
    return pl.pallas_call(
        functools.partial(_flash_kernel, bq=bq),
        out_shape=jax.ShapeDtypeStruct((batch * seq, N_HEADS * HEAD_DIM), BF16),
        grid=(batch, npair, nq),
        in_specs=[pl.BlockSpec((1, 2, bq, LANES), lambda b, hp, i: (b, hp, i, 0)), whole, whole],
        out_specs=pl.BlockSpec((bq, LANES), lambda b, hp, i: (b * nq + i, hp)),
        scratch_shapes=[pltpu.VMEM((2, 2, bq, bq), F32),
                        pltpu.VMEM((2, bq, 1), F32),
                        pltpu.VMEM((2, bq, LANES), F32)],
        compiler_params=_params("parallel", "parallel", "arbitrary"),
        name="flash",
    )(qa, ka, va)


def _mix_residual(x_ref, ys_ref, ya_ref, ga_ref, wt_ref, wb_ref):
    ya = _rms(ya_ref[...].astype(F32), ga_ref[...]).astype(BF16)
    return (x_ref[...]
            + jnp.dot(ys_ref[...], wt_ref[...], preferred_element_type=F32)
            + jnp.dot(ya, wb_ref[...], preferred_element_type=F32))


def _ple_add(x, p_ref, g_ref, wg_ref, wp_ref):
    u = _rms(x, g_ref[...]).astype(BF16)
    gate = 1.0 / (1.0 + jnp.exp(-jnp.dot(u, wg_ref[...], preferred_element_type=F32)))
    pe = jnp.dot(p_ref[...].astype(BF16), wp_ref[...], preferred_element_type=F32)
    return x + gate * pe


def _const_spec(a):
    nd = a.ndim
    return pl.BlockSpec(a.shape, lambda *_: (0,) * nd, pipeline_mode=pl.Buffered(1))


def _split_rows(ref, x):
    for s in range(SUBLANES):
        ref[:, s, :] = x[:, s * LANES:(s + 1) * LANES]


def _merge_rows(ref):
    return jnp.concatenate([ref[:, s, :] for s in range(SUBLANES)], axis=1)


def _ffn_chunks(d_ff, step):
    return [(f0, min(step, d_ff - f0)) for f0 in range(0, d_ff, step)]


def _swiglu(u, wg, wu, wd, fc):
    y = None
    for f0, fw in _ffn_chunks(wg.shape[1], fc):
        hg = jnp.dot(u, wg[:, f0:f0 + fw], preferred_element_type=F32)
        hu = jnp.dot(u, wu[:, f0:f0 + fw], preferred_element_type=F32)
        h = (_silu(hg) * hu).astype(BF16)
        part = jnp.dot(h, wd[f0:f0 + fw, :], preferred_element_type=F32)
        y = part if y is None else y + part
    return y


def _dense_layer_kernel(x_ref, ys_ref, ya_ref, p_ref, ga_ref, wt_ref, wb_ref, g2_ref,
                        wg_ref, wu_ref, wd_ref, gp_ref, wpg_ref, wpp_ref, o_ref, *, fc):
    x1 = _mix_residual(x_ref, ys_ref, ya_ref, ga_ref, wt_ref, wb_ref)
    u = _rms(x1, g2_ref[...]).astype(BF16)
    x2 = x1 + _swiglu(u, wg_ref, wu_ref, wd_ref, fc)
    o_ref[...] = _ple_add(x2, p_ref, gp_ref, wpg_ref, wpp_ref)


def _dense_layer(x2d, y_ssd, y_att, p2d, ga, w_top, w_bot, g2, wg, wu, wd, gp, wpg, wpp,
                 *, tm=512, fc=1024):
    t, d = x2d.shape
    row = lambda a: pl.BlockSpec((tm, a.shape[1]), lambda i: (i, 0))
    consts = (ga, w_top, w_bot, g2, wg, wu, wd, gp, wpg, wpp)
    return pl.pallas_call(
        functools.partial(_dense_layer_kernel, fc=fc),
        out_shape=jax.ShapeDtypeStruct((t, d), F32),
        grid=(t // tm,),
        in_specs=[row(x2d), row(y_ssd), row(y_att), row(p2d)] + [_const_spec(a) for a in consts],
        out_specs=row(x2d),
        compiler_params=_params("parallel"),
        name="dense_layer",
    )(x2d, y_ssd, y_att, p2d, *consts)


def _route_kernel(x_ref, ys_ref, ya_ref, ga_ref, wt_ref, wb_ref, g2_ref, wrh_ref, wrl_ref,
                  x1_ref, u3_ref, meta_ref, cnt_ref, carry_ref, *, n_experts):
    tm = x_ref.shape[0]

    @pl.when(pl.program_id(0) == 0)
    def _():
        carry_ref[...] = jnp.zeros_like(carry_ref)

    x1 = _mix_residual(x_ref, ys_ref, ya_ref, ga_ref, wt_ref, wb_ref)
    x1_ref[...] = x1
    u = _rms(x1, g2_ref[...])
    _split_rows(u3_ref, u)

    u_hi = u.astype(BF16)
    u_lo = (u - u_hi.astype(F32)).astype(BF16)
    logits = (jnp.dot(u_hi, wrh_ref[...], preferred_element_type=F32)
              + jnp.dot(u_lo, wrh_ref[...], preferred_element_type=F32)
              + jnp.dot(u_hi, wrl_ref[...], preferred_element_type=F32))
    lane_i = lax.broadcasted_iota(jnp.int32, (tm, LANES), 1)
    lane = lane_i.astype(F32)
    logits = jnp.where(lane_i < n_experts, logits, -jnp.inf)
    m1 = jnp.max(logits, axis=-1, keepdims=True)
    i1 = jnp.min(jnp.where(logits == m1, lane, float(LANES)), axis=-1, keepdims=True)
    rest = jnp.where(lane == i1, -jnp.inf, logits)
    m2 = jnp.max(rest, axis=-1, keepdims=True)
    i2 = jnp.min(jnp.where(rest == m2, lane, float(LANES)), axis=-1, keepdims=True)
    e2 = jnp.exp(m2 - m1)
    g1 = 1.0 / (1.0 + e2)
    g2 = e2 * g1

    hit = jnp.where(lane == i1, 1.0, jnp.where(lane == i2, 1.0, 0.0))
    incl = _cumsum_rows(hit)
    before = incl - hit + carry_ref[...]
    r1 = jnp.sum(jnp.where(lane == i1, before, 0.0), axis=-1, keepdims=True)
    r2 = jnp.sum(jnp.where(lane == i2, before, 0.0), axis=-1, keepdims=True)
    meta_ref[...] = jnp.where(lane_i == 0, i1, jnp.where(lane_i == 1, i2, jnp.where(
        lane_i == 2, g1, jnp.where(lane_i == 3, g2, jnp.where(
            lane_i == 4, r1, jnp.where(lane_i == 5, r2, 0.0))))))
    total = carry_ref[...] + incl[tm - 1:tm, :]
    carry_ref[...] = total
    cnt_ref[...] = total


def _route(x2d, y_ssd, y_att, ga, w_top, w_bot, g2, wr_hi, wr_lo, *, n_experts, tm):
    t, d = x2d.shape
    row = lambda a: pl.BlockSpec((tm, a.shape[1]), lambda i: (i, 0))
    consts = (ga, w_top, w_bot, g2, wr_hi, wr_lo)
    return pl.pallas_call(
        functools.partial(_route_kernel, n_experts=n_experts),
        out_shape=(jax.ShapeDtypeStruct((t, d), F32),
                   jax.ShapeDtypeStruct((t, SUBLANES, LANES), F32),
                   jax.ShapeDtypeStruct((t, LANES), F32),
                   jax.ShapeDtypeStruct((1, LANES), F32)),
        grid=(t // tm,),
        in_specs=[row(x2d), row(y_ssd), row(y_att)] + [_const_spec(a) for a in consts],
        out_specs=(row(x2d),
                   pl.BlockSpec((tm, SUBLANES, LANES), lambda i: (i, 0, 0)),
                   pl.BlockSpec((tm, LANES), lambda i: (i, 0)),
                   pl.BlockSpec((1, LANES), lambda i: (0, 0))),
        scratch_shapes=[pltpu.VMEM((1, LANES), F32)],
        compiler_params=_params("arbitrary"),
        name="route",
    )(x2d, y_ssd, y_att, *consts)


def _scatter_kernel(pos_ref, u3_ref, init_ref, xs_ref, sem):
    del init_ref
    tm = u3_ref.shape[0]

    def row_copy(r, dst):
        return pltpu.make_async_copy(u3_ref.at[pl.ds(r, 1)], xs_ref.at[pl.ds(dst, 1)], sem)

    def issue(r, carry):
        row_copy(r, pos_ref[0, 0, 2 * r]).start()
        row_copy(r, pos_ref[0, 0, 2 * r + 1]).start()
        return carry

    lax.fori_loop(0, tm, issue, 0, unroll=8)

    def drain(r, carry):
        row_copy(r, 0).wait()
        row_copy(r, 0).wait()
        return carry

    lax.fori_loop(0, tm, drain, 0, unroll=8)


def _scatter(pos_blk, u3, xs_init, *, tm):
    t = u3.shape[0]
    return pl.pallas_call(
        _scatter_kernel,
        out_shape=jax.ShapeDtypeStruct(xs_init.shape, F32),
        grid=(t // tm,),
        in_specs=[pl.BlockSpec((1, 1, TOP_K * tm), lambda i: (i, 0, 0), memory_space=pltpu.SMEM),
                  pl.BlockSpec((tm, SUBLANES, LANES), lambda i: (i, 0, 0)),
                  pl.BlockSpec(memory_space=pl.ANY)],
        out_specs=pl.BlockSpec(memory_space=pl.ANY),
        scratch_shapes=[pltpu.SemaphoreType.DMA(())],
        input_output_aliases={2: 0},
        compiler_params=_params("arbitrary"),
        name="scatter_rows",
    )(pos_blk, u3, xs_init)


def _expert_kernel(te_ref, nu_ref, xs_ref, wg_ref, wu_ref, wd_ref, ys_ref, *, fc):
    used = pl.program_id(0) < nu_ref[0]

    @pl.when(used)
    def _():
        u = _merge_rows(xs_ref).astype(BF16)
        _split_rows(ys_ref, _swiglu(u, wg_ref.at[0], wu_ref.at[0], wd_ref.at[0], fc))

    @pl.when(jnp.logical_not(used))
    def _():
        ys_ref[...] = jnp.zeros_like(ys_ref)


def _experts(tile_expert, n_used, xs, wg, wu, wd, *, tr, fc=768):
    _, d, dff = wg.shape
    rows = pl.BlockSpec((tr, SUBLANES, LANES), lambda g, te, nu: (g, 0, 0))
    return pl.pallas_call(
        functools.partial(_expert_kernel, fc=fc),
        out_shape=jax.ShapeDtypeStruct(xs.shape, F32),
        grid_spec=pltpu.PrefetchScalarGridSpec(
            num_scalar_prefetch=2,
            grid=(xs.shape[0] // tr,),
            in_specs=[rows,
                      pl.BlockSpec((1, d, dff), lambda g, te, nu: (te[g], 0, 0)),
                      pl.BlockSpec((1, d, dff), lambda g, te, nu: (te[g], 0, 0)),
                      pl.BlockSpec((1, dff, d), lambda g, te, nu: (te[g], 0, 0))],
            out_specs=rows),
        compiler_params=_params("arbitrary"),
        name="experts",
    )(tile_expert, n_used, xs, wg, wu, wd)


def _combine_kernel(pos_ref, x1_ref, meta_ref, p_ref, gp_ref, wpg_ref, wpp_ref, ys_ref, o_ref,
                    ybuf_ref, sem):
    tm = x1_ref.shape[0]

    def row_copy(r, k, src):
        return pltpu.make_async_copy(ys_ref.at[pl.ds(src, 1)], ybuf_ref.at[k, pl.ds(r, 1)], sem)

    def issue(r, carry):
        for k in range(TOP_K):
            row_copy(r, k, pos_ref[0, 0, TOP_K * r + k]).start()
        return carry

    lax.fori_loop(0, tm, issue, 0, unroll=8)

    def drain(r, carry):
        for k in range(TOP_K):
            row_copy(r, k, 0).wait()
        return carry

    lax.fori_loop(0, tm, drain, 0, unroll=8)

    meta = meta_ref[...]
    x2 = (x1_ref[...] + meta[:, 2:3] * _merge_rows(ybuf_ref.at[0])
          + meta[:, 3:4] * _merge_rows(ybuf_ref.at[1]))
    o_ref[...] = _ple_add(x2, p_ref, gp_ref, wpg_ref, wpp_ref)


def _combine(pos_blk, x1, meta, p2d, gp, wpg, wpp, ys, *, tm):
    t, d = x1.shape
    row = lambda a: pl.BlockSpec((tm, a.shape[1]), lambda i: (i, 0))
    return pl.pallas_call(
        _combine_kernel,
        out_shape=jax.ShapeDtypeStruct((t, d), F32),
        grid=(t // tm,),
        in_specs=[pl.BlockSpec((1, 1, TOP_K * tm), lambda i: (i, 0, 0), memory_space=pltpu.SMEM),
                  row(x1), row(meta), row(p2d), _const_spec(gp), _const_spec(wpg),
                  _const_spec(wpp), pl.BlockSpec(memory_space=pl.ANY)],
        out_specs=row(x1),
        scratch_shapes=[pltpu.VMEM((TOP_K, tm, SUBLANES, LANES), F32),
                        pltpu.SemaphoreType.DMA(())],
        compiler_params=_params("arbitrary"),
        name="combine_ple",
    )(pos_blk, x1, meta, p2d, gp, wpg, wpp, ys)


def _moe_layer(x2d, y_ssd, y_att, p2d, ga, w_top, w_bot, g2, wr_hi, wr_lo, wg, wu, wd,
               gp, wpg, wpp, *, tm=512, tr=512):
    t = x2d.shape[0]
    n_experts = wg.shape[0]
    x1, u3, meta, cnt = _route(x2d, y_ssd, y_att, ga, w_top, w_bot, g2, wr_hi, wr_lo,
                               n_experts=n_experts, tm=tm)

    n_tiles = t * TOP_K // tr + n_experts
    counts = cnt[0, :n_experts].astype(jnp.int32)
    tiles_e = (counts + tr - 1) // tr
    ends = jnp.cumsum(tiles_e)
    row0 = (ends - tiles_e) * tr
    expert = meta[:, :TOP_K].astype(jnp.int32)
    rank = meta[:, 2 * TOP_K:3 * TOP_K].astype(jnp.int32)
    pos_blk = (row0[expert] + rank).reshape(t // tm, 1, TOP_K * tm)
    n_used = ends[-1:].astype(jnp.int32)
    g = jnp.minimum(jnp.arange(n_tiles, dtype=jnp.int32), n_used - 1)
    tile_expert = jnp.sum(g[:, None] >= ends[None, :], axis=1).astype(jnp.int32)

    xs_init = jnp.zeros((n_tiles * tr, SUBLANES, LANES), F32)
    xs = _scatter(pos_blk, u3, xs_init, tm=tm)
    ys = _experts(tile_expert, n_used, xs, wg, wu, wd, tr=tr)
    return _combine(pos_blk, x1, meta, p2d, gp, wpg, wpp, ys, tm=tm)


def _pad_lanes(v, offset=0):
    out = jnp.zeros((1, LANES), F32)
    return out.at[0, offset:offset + v.shape[0]].set(v.astype(F32))


def kernel(x, p, norm1_g, w_in, conv_w, conv_b, dt_bias, a_log, d_skip, ssd_norm_g, fg_bias, q_norm_g, k_norm_g, attn_norm_g, w_out, norm2_g, w_gate_dense, w_up_dense, w_down_dense, w_router, w_gate_exp, w_up_exp, w_down_exp, ple_norm_g, w_ple_gate, w_ple_proj):
    batch, seq, d_model = x.shape
    depth = w_in.shape[0]
    t = batch * seq
    d_ssd = N_HEADS * HEAD_DIM
    d_att = N_HEADS * HEAD_DIM
    d_conv = conv_w.shape[2]
    n_experts = w_router.shape[2]
    o_xbc = d_ssd
    o_dt = o_xbc + d_conv
    o_q = o_dt + N_HEADS
    o_k = o_q + d_att
    o_v = o_k + d_att
    o_f = o_v + d_att
    q_blk = (d_conv + d_ssd) // d_att

    ones_bd = (jnp.arange(d_att)[:, None] // HEAD_DIM
               == jnp.arange(d_att)[None, :] // HEAD_DIM).astype(BF16)
    row = lambda v: v.astype(F32).reshape(1, -1)

    xf = x.reshape(t, d_model)
    for i in range(depth):
        w = w_in[i]
        w_main = jnp.concatenate(
            [w[:, o_xbc:o_dt], w[:, :d_ssd], w[:, o_q:o_k], w[:, o_k:o_v], w[:, o_v:o_f]],
            axis=1).astype(BF16)
        w_small = jnp.zeros((d_model, LANES), F32)
        w_small = w_small.at[:, :N_HEADS].set(w[:, o_dt:o_q])
        w_small = w_small.at[:, N_HEADS:2 * N_HEADS].set(w[:, o_f:o_f + N_HEADS]).astype(BF16)

        main, small = _in_proj(xf, row(norm1_g[i]), w_main, w_small)

        y_ssd = _ssd(main, small, conv_w[i].astype(F32), row(conv_b[i]),
                     _pad_lanes(dt_bias[i]), _pad_lanes(a_log[i]),
                     row(jnp.repeat(d_skip[i], HEAD_DIM)), row(ssd_norm_g[i]),
                     batch=batch, seq=seq)

        gq = row(jnp.tile(q_norm_g[i], N_HEADS)) * (LOG2E * HEAD_DIM ** -0.5)
        gk = row(jnp.tile(k_norm_g[i], N_HEADS))
        qa, ka, va = _attn_prep(main, small, _pad_lanes(fg_bias[i], N_HEADS), gq, gk, ones_bd,
                                batch=batch, seq=seq, q_blk=q_blk)
        y_att = _flash(qa, ka, va, batch=batch, seq=seq)

        wo = w_out[i].astype(BF16)
        mix = (xf, y_ssd, y_att, p[i].reshape(t, -1), row(attn_norm_g[i]), wo[:d_ssd], wo[d_ssd:],
               row(norm2_g[i]))
        ple = (row(ple_norm_g[i]), w_ple_gate[i].astype(BF16), w_ple_proj[i].astype(BF16))
        j = i // 2
        if i % 2 == 0:
            xf = _dense_layer(*mix, w_gate_dense[j].astype(BF16), w_up_dense[j].astype(BF16),
                              w_down_dense[j].astype(BF16), *ple)
        else:
            wr = jnp.zeros((d_model, LANES), F32).at[:, :n_experts].set(w_router[j])
            wr_hi = wr.astype(BF16)
            wr_lo = (wr - wr_hi.astype(F32)).astype(BF16)
            xf = _moe_layer(*mix, wr_hi, wr_lo, w_gate_exp[j].astype(BF16),
                            w_up_exp[j].astype(BF16), w_down_exp[j].astype(BF16), *ple)
    return xf.reshape(batch, seq, d_model)
```

```python
import functools

import jax
import jax.numpy as jnp
from jax import lax
from jax.experimental import pallas as pl
from jax.experimental.pallas import tpu as pltpu

F32 = jnp.float32
BF16 = jnp.bfloat16

EPS = 1e-6
LOG2E = 1.4426950408889634
HEAD_DIM = 64
N_HEADS = 8
SSD_GROUPS = 2
D_STATE = 128
CONV_K = 4
CHUNK = 128
TOP_K = 2
LANES = 128
SUBLANES = 8
VMEM_LIMIT = 52 * 1024 * 1024


def _params(*sem):
    return pltpu.CompilerParams(dimension_semantics=sem, vmem_limit_bytes=VMEM_LIMIT)


def _rms(x, g):
    return x * lax.rsqrt(jnp.mean(x * x, axis=-1, keepdims=True) + EPS) * g


def _silu(x):
    return x / (1.0 + jnp.exp(-x))


def _softplus(x):
    return jnp.maximum(x, 0.0) + jnp.log(1.0 + jnp.exp(-jnp.abs(x)))


def _cumsum_rows(x):
    n = x.shape[0]
    row = lax.broadcasted_iota(jnp.int32, x.shape, 0)
    s = 1
    while s < n:
        x = x + jnp.where(row >= s, pltpu.roll(x, s, 0), 0.0)
        s *= 2
    return x


def _expand_heads(v, n_rows):
    lane = lax.broadcasted_iota(jnp.int32, (n_rows, LANES), 1)
    blocks = []
    for j in range(N_HEADS // 2):
        a = jnp.broadcast_to(v[:, 2 * j:2 * j + 1], (n_rows, LANES))
        b = jnp.broadcast_to(v[:, 2 * j + 1:2 * j + 2], (n_rows, LANES))
        blocks.append(jnp.where(lane < HEAD_DIM, a, b))
    return jnp.concatenate(blocks, axis=1)


def _in_proj_kernel(x_ref, g_ref, wm_ref, ws_ref, main_ref, small_ref, *, tn):
    u = _rms(x_ref[...], g_ref[...]).astype(BF16)
    n_main = main_ref.shape[1]
    for n0 in range(0, n_main, tn):
        main_ref[:, n0:n0 + tn] = jnp.dot(
            u, wm_ref[:, n0:n0 + tn], preferred_element_type=F32).astype(BF16)
    small_ref[...] = jnp.dot(u, ws_ref[...], preferred_element_type=F32)


def _in_proj(x2d, g, w_main, w_small, *, tm=512, tn=512):
    t, d = x2d.shape
    n_main = w_main.shape[1]
    return pl.pallas_call(
        functools.partial(_in_proj_kernel, tn=tn),
        out_shape=(jax.ShapeDtypeStruct((t, n_main), BF16),
                   jax.ShapeDtypeStruct((t, LANES), F32)),
        grid=(t // tm,),
        in_specs=[pl.BlockSpec((tm, d), lambda i: (i, 0)),
                  pl.BlockSpec((1, d), lambda i: (0, 0)),
                  pl.BlockSpec((d, n_main), lambda i: (0, 0)),
                  pl.BlockSpec((d, LANES), lambda i: (0, 0))],
        out_specs=(pl.BlockSpec((tm, n_main), lambda i: (i, 0)),
                   pl.BlockSpec((tm, LANES), lambda i: (i, 0))),
        compiler_params=_params("parallel"),
        name="in_proj",
    )(x2d, g, w_main, w_small)


def _ssd_kernel(xbc_ref, z_ref, sm_ref, cw_ref, cb_ref, dtb_ref, alog_ref, dskip_ref, g_ref,
                y_ref, state_ref, cbuf_ref, *, lb):
    d_ssd = N_HEADS * HEAD_DIM
    gw = d_ssd // SSD_GROUPS

    @pl.when(pl.program_id(1) == 0)
    def _():
        state_ref[...] = jnp.zeros_like(state_ref)
        cbuf_ref[0:SUBLANES, :] = jnp.zeros((SUBLANES, cbuf_ref.shape[1]), F32)

    xbc = xbc_ref[...].astype(F32)
    cbuf_ref[SUBLANES:, :] = xbc
    cw = cw_ref[...]
    conv = cb_ref[...] + cw[CONV_K - 1:CONV_K, :] * xbc
    for k in range(CONV_K - 1):
        off = SUBLANES - (CONV_K - 1) + k
        conv = conv + cw[k:k + 1, :] * cbuf_ref[off:off + lb, :]
    cbuf_ref[0:SUBLANES, :] = xbc[lb - SUBLANES:, :]
    act = _silu(conv)

    a_neg = -jnp.exp(alog_ref[...])
    dtb = dtb_ref[...]
    dskip = dskip_ref[...]
    gain = g_ref[...]
    tri = (lax.broadcasted_iota(jnp.int32, (CHUNK, CHUNK), 0)
           >= lax.broadcasted_iota(jnp.int32, (CHUNK, CHUNK), 1))
    lane = lax.broadcasted_iota(jnp.int32, (CHUNK, LANES), 1)

    for c in range(lb // CHUNK):
        r0 = c * CHUNK
        xs = act[r0:r0 + CHUNK, 0:d_ssd]
        bm = act[r0:r0 + CHUNK, d_ssd:d_ssd + SSD_GROUPS * D_STATE].astype(BF16)
        cm = act[r0:r0 + CHUNK, d_ssd + SSD_GROUPS * D_STATE:].astype(BF16)
        dt = _softplus(sm_ref[r0:r0 + CHUNK, :] + dtb)
        acum = _cumsum_rows(dt * a_neg)
        acum_t = acum.T
        total = acum[CHUNK - 1:CHUNK, :]

        dt_e = _expand_heads(dt, CHUNK)
        in_decay_e = _expand_heads(jnp.exp(acum), CHUNK)
        out_decay_e = _expand_heads(jnp.exp(total - acum), CHUNK)
        chunk_decay_e = _expand_heads(jnp.exp(total), 1)

        xdt = xs * dt_e
        xdt_b = xdt.astype(BF16)
        w_state = (xdt * out_decay_e).astype(BF16)

        y_parts = []
        for g in range(SSD_GROUPS):
            bg = bm[:, g * D_STATE:(g + 1) * D_STATE]
            cg = cm[:, g * D_STATE:(g + 1) * D_STATE]
            cb = lax.dot_general(cg, bg, (((1,), (1,)), ((), ())),
                                 preferred_element_type=F32)
            st = state_ref[g]
            y_off = jnp.dot(cg, st.astype(BF16), preferred_element_type=F32)
            for pr in range(gw // LANES):
                pair = []
                col0 = g * gw + pr * LANES
                for hh in range(2):
                    h = (col0 // HEAD_DIM) + hh
                    seg = acum[:, h:h + 1] - acum_t[h:h + 1, :]
                    m = (cb * jnp.where(tri, jnp.exp(seg), 0.0)).astype(BF16)
                    pair.append(jnp.dot(m, xdt_b[:, col0:col0 + LANES],
                                        preferred_element_type=F32))
                y_parts.append(jnp.where(lane < HEAD_DIM, pair[0], pair[1])
                               + y_off[:, pr * LANES:(pr + 1) * LANES]
                               * in_decay_e[:, col0:col0 + LANES])
            upd = lax.dot_general(bg, w_state[:, g * gw:(g + 1) * gw], (((0,), (0,)), ((), ())),
                                  preferred_element_type=F32)
            state_ref[g] = st * chunk_decay_e[:, g * gw:(g + 1) * gw] + upd

        y = jnp.concatenate(y_parts, axis=1) + dskip * xs
        y = y * _silu(z_ref[r0:r0 + CHUNK, :].astype(F32))
        y_ref[r0:r0 + CHUNK, :] = _rms(y, gain).astype(BF16)


def _ssd(main, small, conv_w, conv_b, dtb, alog, dskip_e, gain, *, batch, seq, lb=512):
    t = batch * seq
    nj = seq // lb
    d_ssd = N_HEADS * HEAD_DIM
    d_conv = conv_w.shape[1]
    z_blk = d_conv // d_ssd
    vec = lambda n: pl.BlockSpec((1, n), lambda b, j: (0, 0))
    return pl.pallas_call(
        functools.partial(_ssd_kernel, lb=lb),
        out_shape=jax.ShapeDtypeStruct((t, d_ssd), BF16),
        grid=(batch, nj),
        in_specs=[pl.BlockSpec((lb, d_conv), lambda b, j: (b * nj + j, 0)),
                  pl.BlockSpec((lb, d_ssd), lambda b, j: (b * nj + j, z_blk)),
                  pl.BlockSpec((lb, LANES), lambda b, j: (b * nj + j, 0)),
                  pl.BlockSpec((CONV_K, d_conv), lambda b, j: (0, 0)),
                  vec(d_conv), vec(LANES), vec(LANES), vec(d_ssd), vec(d_ssd)],
        out_specs=pl.BlockSpec((lb, d_ssd), lambda b, j: (b * nj + j, 0)),
        scratch_shapes=[pltpu.VMEM((SSD_GROUPS, D_STATE, d_ssd // SSD_GROUPS), F32),
                        pltpu.VMEM((SUBLANES + lb, d_conv), F32)],
        compiler_params=_params("parallel", "arbitrary"),
        name="ssd",
    )(main, main, small, conv_w, conv_b, dtb, alog, dskip_e, gain)


def _attn_prep_kernel(q_ref, k_ref, v_ref, sm_ref, fgb_ref, gq_ref, gk_ref, ones_ref,
                      qa_ref, ka_ref, va_ref, carry_ref, *, ts):
    @pl.when(pl.program_id(1) == 0)
    def _():
        carry_ref[...] = jnp.zeros_like(carry_ref)

    def headnorm(ref, g):
        v = ref[...].astype(F32)
        ssq = jnp.dot((v * v).astype(BF16), ones_ref[...], preferred_element_type=F32)
        return v * lax.rsqrt(ssq * (1.0 / HEAD_DIM) + EPS) * g

    qn = headnorm(q_ref, gq_ref[...])
    kn = headnorm(k_ref, gk_ref[...])
    vv = v_ref[...].astype(F32)

    xf = sm_ref[...] + fgb_ref[...]
    log_f = jnp.minimum(xf, 0.0) - jnp.log(1.0 + jnp.exp(-jnp.abs(xf)))
    fcum = _cumsum_rows(log_f) + carry_ref[...]
    carry_ref[...] = fcum[ts - 1:ts, :]
    f2 = fcum * LOG2E
    f_hi = f2.astype(BF16).astype(F32)
    rem = f2 - f_hi
    f_mid = rem.astype(BF16).astype(F32)
    f_lo = rem - f_mid

    lane = lax.broadcasted_iota(jnp.int32, (ts, LANES), 1)
    d = HEAD_DIM
    v_tail = jnp.where(lane == d, 1.0, 0.0)
    for h in range(N_HEADS):
        j = h // 2
        qb = qn[:, j * LANES:(j + 1) * LANES]
        kb = kn[:, j * LANES:(j + 1) * LANES]
        vb = vv[:, j * LANES:(j + 1) * LANES]
        if h % 2:
            qb = pltpu.roll(qb, HEAD_DIM, 1)
            kb = pltpu.roll(kb, HEAD_DIM, 1)
            vb = pltpu.roll(vb, HEAD_DIM, 1)
        va_ref[0, h] = jnp.where(lane < d, vb, v_tail).astype(BF16)
        fl = N_HEADS + h
        bh = jnp.broadcast_to(f_hi[:, fl:fl + 1], (ts, LANES))
        bm = jnp.broadcast_to(f_mid[:, fl:fl + 1], (ts, LANES))
        bl = jnp.broadcast_to(f_lo[:, fl:fl + 1], (ts, LANES))
        qx = jnp.where(lane == d, bh, jnp.where(lane == d + 1, bm, jnp.where(
            lane == d + 2, bl, jnp.where(lane < d + 6, 1.0, 0.0))))
        kx = jnp.where(lane < d + 3, 1.0, jnp.where(lane == d + 3, -bh, jnp.where(
            lane == d + 4, -bm, jnp.where(lane == d + 5, -bl, 0.0))))
        qa_ref[0, h] = jnp.where(lane < d, qb, qx).astype(BF16)
        ka_ref[0, h] = jnp.where(lane < d, kb, kx).astype(BF16)


def _attn_prep(main, small, fgb, gq, gk, ones_bd, *, batch, seq, q_blk, ts=512):
    nj = seq // ts
    d_att = N_HEADS * HEAD_DIM
    vec = lambda n: pl.BlockSpec((1, n), lambda b, j: (0, 0))
    aug = jax.ShapeDtypeStruct((batch, N_HEADS, seq, LANES), BF16)
    aug_spec = pl.BlockSpec((1, N_HEADS, ts, LANES), lambda b, j: (b, 0, j, 0))
    col = lambda c: pl.BlockSpec((ts, d_att), lambda b, j: (b * nj + j, c))
    return pl.pallas_call(
        functools.partial(_attn_prep_kernel, ts=ts),
        out_shape=(aug, aug, aug),
        grid=(batch, nj),
        in_specs=[col(q_blk), col(q_blk + 1), col(q_blk + 2),
                  pl.BlockSpec((ts, LANES), lambda b, j: (b * nj + j, 0)),
                  vec(LANES), vec(d_att), vec(d_att),
                  pl.BlockSpec((d_att, d_att), lambda b, j: (0, 0))],
        out_specs=(aug_spec, aug_spec, aug_spec),
        scratch_shapes=[pltpu.VMEM((1, LANES), F32)],
        compiler_params=_params("parallel", "arbitrary"),
        name="attn_prep",
    )(main, main, main, small, fgb, gq, gk, ones_bd)


def _flash_kernel(qa_ref, ka_ref, va_ref, o_ref, s_ref, m_ref, acc_ref, *, bq):
    i = pl.program_id(2)
    lane = lax.broadcasted_iota(jnp.int32, (bq, LANES), 1)
    tri = (lax.broadcasted_iota(jnp.int32, (bq, bq), 0)
           >= lax.broadcasted_iota(jnp.int32, (bq, bq), 1))

    def logits(kb, slot):
        k0 = pl.multiple_of(kb * bq, bq)
        for hh in range(2):
            s_ref[slot, hh] = lax.dot_general(
                qa_ref[0, hh], ka_ref[0, hh, pl.ds(k0, bq), :], (((1,), (1,)), ((), ())),
                preferred_element_type=F32)

    def consume(kb, slot, masked):
        k0 = pl.multiple_of(kb * bq, bq)
        for hh in range(2):
            s = s_ref[slot, hh]
            if masked:
                s = jnp.where(tri, s, -jnp.inf)
            m = m_ref[hh]
            m_new = jnp.maximum(m, jnp.max(s, axis=-1, keepdims=True))
            p = jnp.exp2((s - m_new).astype(BF16))
            acc_ref[hh] = (jnp.exp2(m - m_new) * acc_ref[hh]
                           + jnp.dot(p, va_ref[0, hh, pl.ds(k0, bq), :],
                                     preferred_element_type=F32))
            m_ref[hh] = m_new

    m_ref[...] = jnp.full(m_ref.shape, -jnp.inf, F32)
    acc_ref[...] = jnp.zeros(acc_ref.shape, F32)
    logits(0, 0)

    def pair(j, carry):
        kb = 2 * j
        logits(kb + 1, 1)
        consume(kb, 0, False)
        logits(kb + 2, 0)
        consume(kb + 1, 1, False)
        return carry

    lax.fori_loop(0, i // 2, pair, 0)

    @pl.when(i % 2 == 0)
    def _():
        consume(i, 0, True)

    @pl.when(i % 2 == 1)
    def _():
        logits(i, 1)
        consume(i - 1, 0, False)
        consume(i, 1, True)

    acc0 = acc_ref[0]
    acc1 = acc_ref[1]
    o0 = acc0 / acc0[:, HEAD_DIM:HEAD_DIM + 1]
    o1 = acc1 / acc1[:, HEAD_DIM:HEAD_DIM + 1]
    o_ref[...] = jnp.where(lane < HEAD_DIM, o0, pltpu.roll(o1, HEAD_DIM, 1)).astype(BF16)


def _flash(qa, ka, va, *, batch, seq, bq=512):
    nq = seq // bq
    npair = N_HEADS // 2
    whole = pl.BlockSpec((1, 2, seq, LANES), lambda b, hp, i: (b, hp, 0, 0))
    return pl.pallas_call(
        functools.partial(_flash_kernel, bq=bq),
        out_shape=jax.ShapeDtypeStruct((batch * seq, N_HEADS * HEAD_DIM), BF16),
        grid=(batch, npair, nq),
        in_specs=[pl.BlockSpec((1, 2, bq, LANES), lambda b, hp, i: (b, hp, i, 0)), whole, whole],
        out_specs=pl.BlockSpec((bq, LANES), lambda b, hp, i: (b * nq + i, hp)),
        scratch_shapes=[pltpu.VMEM((2, 2, bq, bq), F32),
                        pltpu.VMEM((2, bq, 1), F32),
                        pltpu.VMEM((2, bq, LANES), F32)],
        compiler_params=_params("parallel", "parallel", "arbitrary"),
        name="flash",
    )(qa, ka, va)


def _mix_residual(x_ref, ys_ref, ya_ref, ga_ref, wt_ref, wb_ref):
    ya = _rms(ya_ref[...].astype(F32), ga_ref[...]).astype(BF16)
    return (x_ref[...]
            + jnp.dot(ys_ref[...], wt_ref[...], preferred_element_type=F32)
            + jnp.dot(ya, wb_ref[...], preferred_element_type=F32))


def _ple_add(x, p_ref, g_ref, wg_ref, wp_ref):
    u = _rms(x, g_ref[...]).astype(BF16)
    gate = 1.0 / (1.0 + jnp.exp(-jnp.dot(u, wg_ref[...], preferred_element_type=F32)))
    pe = jnp.dot(p_ref[...].astype(BF16), wp_ref[...], preferred_element_type=F32)
    return x + gate * pe


def _const_spec(a):
    nd = a.ndim
    return pl.BlockSpec(a.shape, lambda *_: (0,) * nd, pipeline_mode=pl.Buffered(1))


def _split_rows(ref, x):
    for s in range(SUBLANES):
        ref[:, s, :] = x[:, s * LANES:(s + 1) * LANES]


def _merge_rows(ref):
    return jnp.concatenate([ref[:, s, :] for s in range(SUBLANES)], axis=1)


def _ffn_chunks(d_ff, step):
    return [(f0, min(step, d_ff - f0)) for f0 in range(0, d_ff, step)]


def _swiglu(u, wg, wu, wd, fc):
    y = None
    for f0, fw in _ffn_chunks(wg.shape[1], fc):
        hg = jnp.dot(u, wg[:, f0:f0 + fw], preferred_element_type=F32)
        hu = jnp.dot(u, wu[:, f0:f0 + fw], preferred_element_type=F32)
        h = (_silu(hg) * hu).astype(BF16)
        part = jnp.dot(h, wd[f0:f0 + fw, :], preferred_element_type=F32)
        y = part if y is None else y + part
    return y


def _dense_layer_kernel(x_ref, ys_ref, ya_ref, p_ref, ga_ref, wt_ref, wb_ref, g2_ref,
                        wg_ref, wu_ref, wd_ref, gp_ref, wpg_ref, wpp_ref, o_ref, *, fc):
    x1 = _mix_residual(x_ref, ys_ref, ya_ref, ga_ref, wt_ref, wb_ref)
    u = _rms(x1, g2_ref[...]).astype(BF16)
    x2 = x1 + _swiglu(u, wg_ref, wu_ref, wd_ref, fc)
    o_ref[...] = _ple_add(x2, p_ref, gp_ref, wpg_ref, wpp_ref)


def _dense_layer(x2d, y_ssd, y_att, p2d, ga, w_top, w_bot, g2, wg, wu, wd, gp, wpg, wpp,
                 *, tm=512, fc=1024):
    t, d = x2d.shape
    row = lambda a: pl.BlockSpec((tm, a.shape[1]), lambda i: (i, 0))
    consts = (ga, w_top, w_bot, g2, wg, wu, wd, gp, wpg, wpp)
    return pl.pallas_call(
        functools.partial(_dense_layer_kernel, fc=fc),
        out_shape=jax.ShapeDtypeStruct((t, d), F32),
        grid=(t // tm,),
        in_specs=[row(x2d), row(y_ssd), row(y_att), row(p2d)] + [_const_spec(a) for a in consts],
        out_specs=row(x2d),
        compiler_params=_params("parallel"),
        name="dense_layer",
    )(x2d, y_ssd, y_att, p2d, *consts)


def _route_kernel(x_ref, ys_ref, ya_ref, ga_ref, wt_ref, wb_ref, g2_ref, wrh_ref, wrl_ref,
                  x1_ref, u3_ref, meta_ref, cnt_ref, carry_ref, *, n_experts):
    tm = x_ref.shape[0]

    @pl.when(pl.program_id(0) == 0)
    def _():
        carry_ref[...] = jnp.zeros_like(carry_ref)

    x1 = _mix_residual(x_ref, ys_ref, ya_ref, ga_ref, wt_ref, wb_ref)
    x1_ref[...] = x1
    u = _rms(x1, g2_ref[...])
    _split_rows(u3_ref, u)

    u_hi = u.astype(BF16)
    u_lo = (u - u_hi.astype(F32)).astype(BF16)
    logits = (jnp.dot(u_hi, wrh_ref[...], preferred_element_type=F32)
              + jnp.dot(u_lo, wrh_ref[...], preferred_element_type=F32)
              + jnp.dot(u_hi, wrl_ref[...], preferred_element_type=F32))
    lane_i = lax.broadcasted_iota(jnp.int32, (tm, LANES), 1)
    lane = lane_i.astype(F32)
    logits = jnp.where(lane_i < n_experts, logits, -jnp.inf)
    m1 = jnp.max(logits, axis=-1, keepdims=True)
    i1 = jnp.min(jnp.where(logits == m1, lane, float(LANES)), axis=-1, keepdims=True)
    rest = jnp.where(lane == i1, -jnp.inf, logits)
    m2 = jnp.max(rest, axis=-1, keepdims=True)
    i2 = jnp.min(jnp.where(rest == m2, lane, float(LANES)), axis=-1, keepdims=True)
    e2 = jnp.exp(m2 - m1)
    g1 = 1.0 / (1.0 + e2)
    g2 = e2 * g1

    hit = jnp.where(lane == i1, 1.0, jnp.where(lane == i2, 1.0, 0.0))
    incl = _cumsum_rows(hit)
    before = incl - hit + carry_ref[...]
    r1 = jnp.sum(jnp.where(lane == i1, before, 0.0), axis=-1, keepdims=True)
    r2 = jnp.sum(jnp.where(lane == i2, before, 0.0), axis=-1, keepdims=True)
    meta_ref[...] = jnp.where(lane_i == 0, i1, jnp.where(lane_i == 1, i2, jnp.where(
        lane_i == 2, g1, jnp.where(lane_i == 3, g2, jnp.where(
            lane_i == 4, r1, jnp.where(lane_i == 5, r2, 0.0))))))
    total = carry_ref[...] + incl[tm - 1:tm, :]
    carry_ref[...] = total
    cnt_ref[...] = total


def _route(x2d, y_ssd, y_att, ga, w_top, w_bot, g2, wr_hi, wr_lo, *, n_experts, tm):
    t, d = x2d.shape
    row = lambda a: pl.BlockSpec((tm, a.shape[1]), lambda i: (i, 0))
    consts = (ga, w_top, w_bot, g2, wr_hi, wr_lo)
    return pl.pallas_call(
        functools.partial(_route_kernel, n_experts=n_experts),
        out_shape=(jax.ShapeDtypeStruct((t, d), F32),
                   jax.ShapeDtypeStruct((t, SUBLANES, LANES), F32),
                   jax.ShapeDtypeStruct((t, LANES), F32),
                   jax.ShapeDtypeStruct((1, LANES), F32)),
        grid=(t // tm,),
        in_specs=[row(x2d), row(y_ssd), row(y_att)] + [_const_spec(a) for a in consts],
        out_specs=(row(x2d),
                   pl.BlockSpec((tm, SUBLANES, LANES), lambda i: (i, 0, 0)),
                   pl.BlockSpec((tm, LANES), lambda i: (i, 0)),
                   pl.BlockSpec((1, LANES), lambda i: (0, 0))),
        scratch_shapes=[pltpu.VMEM((1, LANES), F32)],
        compiler_params=_params("arbitrary"),
        name="route",
    )(x2d, y_ssd, y_att, *consts)


def _scatter_kernel(pad0_ref, padn_ref, nu_ref, pos_ref, u3_ref, xs_ref, zero_ref, sem,
                    *, n_experts, tr):
    tm = u3_ref.shape[0]

    def row_copy(r, dst):
        return pltpu.make_async_copy(u3_ref.at[pl.ds(r, 1)], xs_ref.at[pl.ds(dst, 1)], sem)

    def issue(r, carry):
        for k in range(TOP_K):
            row_copy(r, pos_ref[0, 0, TOP_K * r + k]).start(priority=k)
        return carry

    lax.fori_loop(0, tm, issue, 0, unroll=8)

    def drain(r, carry):
        for k in range(TOP_K):
            row_copy(r, 0).wait()
        return carry

    lax.fori_loop(0, tm, drain, 0, unroll=8)

    @pl.when(pl.program_id(0) == pl.num_programs(0) - 1)
    def _():
        zero_ref[...] = jnp.zeros_like(zero_ref)

        def zero_row(dst):
            return pltpu.make_async_copy(zero_ref.at[pl.ds(0, 1)], xs_ref.at[pl.ds(dst, 1)], sem)

        def zero_tile(g):
            return pltpu.make_async_copy(zero_ref, xs_ref.at[pl.ds(g * tr, tr)], sem)

        def loop(lo, hi, fn):
            lax.fori_loop(lo, hi, lambda j, c: (fn(j), c)[1], 0)

        for e in range(n_experts):
            loop(0, padn_ref[e], lambda j: zero_row(pad0_ref[e] + j).start())
            loop(0, padn_ref[e], lambda j: zero_row(0).wait())
        n_tiles = xs_ref.shape[0] // tr
        loop(nu_ref[0], n_tiles, lambda g: zero_tile(g).start())
        loop(nu_ref[0], n_tiles, lambda g: zero_tile(0).wait())


def _scatter(pad0, padn, n_used, pos_blk, u3, *, n_tiles, tr, tm):
    t = u3.shape[0]
    return pl.pallas_call(
        functools.partial(_scatter_kernel, n_experts=pad0.shape[0], tr=tr),
        out_shape=jax.ShapeDtypeStruct((n_tiles * tr, SUBLANES, LANES), F32),
        grid_spec=pltpu.PrefetchScalarGridSpec(
            num_scalar_prefetch=3,
            grid=(t // tm,),
            in_specs=[pl.BlockSpec((1, 1, TOP_K * tm), lambda i, *_: (i, 0, 0),
                                   memory_space=pltpu.SMEM),
                      pl.BlockSpec((tm, SUBLANES, LANES), lambda i, *_: (i, 0, 0))],
            out_specs=pl.BlockSpec(memory_space=pl.ANY),
            scratch_shapes=[pltpu.VMEM((tr, SUBLANES, LANES), F32), pltpu.SemaphoreType.DMA(())]),
        compiler_params=_params("arbitrary"),
        name="scatter_rows",
    )(pad0, padn, n_used, pos_blk, u3)


def _expert_kernel(te_ref, nu_ref, xs_ref, wg_ref, wu_ref, wd_ref, ys_ref, *, fc):
    used = pl.program_id(0) < nu_ref[0]

    @pl.when(used)
    def _():
        u = _merge_rows(xs_ref).astype(BF16)
        _split_rows(ys_ref, _swiglu(u, wg_ref.at[0], wu_ref.at[0], wd_ref.at[0], fc))

    @pl.when(jnp.logical_not(used))
    def _():
        ys_ref[...] = jnp.zeros_like(ys_ref)


def _experts(tile_expert, n_used, xs, wg, wu, wd, *, tr, fc=768):
    _, d, dff = wg.shape
    rows = pl.BlockSpec((tr, SUBLANES, LANES), lambda g, te, nu: (g, 0, 0))
    return pl.pallas_call(
        functools.partial(_expert_kernel, fc=fc),
        out_shape=jax.ShapeDtypeStruct(xs.shape, F32),
        grid_spec=pltpu.PrefetchScalarGridSpec(
            num_scalar_prefetch=2,
            grid=(xs.shape[0] // tr,),
            in_specs=[pl.BlockSpec((tr, SUBLANES, LANES),
                                   lambda g, te, nu: (jnp.minimum(g, nu[0] - 1), 0, 0)),
                      pl.BlockSpec((1, d, dff), lambda g, te, nu: (te[g], 0, 0)),
                      pl.BlockSpec((1, d, dff), lambda g, te, nu: (te[g], 0, 0)),
                      pl.BlockSpec((1, dff, d), lambda g, te, nu: (te[g], 0, 0))],
            out_specs=rows),
        compiler_params=_params("arbitrary"),
        name="experts",
    )(tile_expert, n_used, xs, wg, wu, wd)


def _combine_kernel(pos_ref, nxt_ref, x1_ref, meta_ref, p_ref, gp_ref, wpg_ref, wpp_ref, ys_ref,
                    o_ref, ybuf_ref, sem):
    tm = x1_ref.shape[0]
    i = pl.program_id(0)
    slot = lax.rem(i, 2)

    def row_copy(sl, r, k, src):
        return pltpu.make_async_copy(ys_ref.at[pl.ds(src, 1)], ybuf_ref.at[sl, k, pl.ds(r, 1)],
                                     sem.at[sl])

    def request(rows_ref, sl):
        def issue(r, carry):
            for k in range(TOP_K):
                row_copy(sl, r, k, rows_ref[0, 0, TOP_K * r + k]).start(priority=k)
            return carry
        lax.fori_loop(0, tm, issue, 0, unroll=8)

    @pl.when(i == 0)
    def _():
        request(pos_ref, 0)

    @pl.when(i + 1 < pl.num_programs(0))
    def _():
        request(nxt_ref, 1 - slot)

    def drain(r, carry):
        for k in range(TOP_K):
            row_copy(slot, r, k, 0).wait()
        return carry

    lax.fori_loop(0, tm, drain, 0, unroll=8)

    meta = meta_ref[...]
    x2 = (x1_ref[...] + meta[:, 2:3] * _merge_rows(ybuf_ref.at[slot, 0])
          + meta[:, 3:4] * _merge_rows(ybuf_ref.at[slot, 1]))
    o_ref[...] = _ple_add(x2, p_ref, gp_ref, wpg_ref, wpp_ref)


def _combine(pos_blk, x1, meta, p2d, gp, wpg, wpp, ys, *, tm):
    t, d = x1.shape
    nt = t // tm
    row = lambda a: pl.BlockSpec((tm, a.shape[1]), lambda i: (i, 0))
    rows_of = lambda f: pl.BlockSpec((1, 1, TOP_K * tm), lambda i: (f(i), 0, 0),
                                     memory_space=pltpu.SMEM)
    return pl.pallas_call(
        _combine_kernel,
        out_shape=jax.ShapeDtypeStruct((t, d), F32),
        grid=(nt,),
        in_specs=[rows_of(lambda i: i), rows_of(lambda i: jnp.minimum(i + 1, nt - 1)),
                  row(x1), row(meta), row(p2d), _const_spec(gp), _const_spec(wpg),
                  _const_spec(wpp), pl.BlockSpec(memory_space=pl.ANY)],
        out_specs=row(x1),
        scratch_shapes=[pltpu.VMEM((2, TOP_K, tm, SUBLANES, LANES), F32),
                        pltpu.SemaphoreType.DMA((2,))],
        compiler_params=_params("arbitrary"),
        name="combine_ple",
    )(pos_blk, pos_blk, x1, meta, p2d, gp, wpg, wpp, ys)


def _moe_layer(x2d, y_ssd, y_att, p2d, ga, w_top, w_bot, g2, wr_hi, wr_lo, wg, wu, wd,
               gp, wpg, wpp, *, tm=512, tr=512):
    t = x2d.shape[0]
    n_experts = wg.shape[0]
    x1, u3, meta, cnt = _route(x2d, y_ssd, y_att, ga, w_top, w_bot, g2, wr_hi, wr_lo,
                               n_experts=n_experts, tm=tm)

    n_tiles = t * TOP_K // tr + n_experts
    counts = cnt[0, :n_experts].astype(jnp.int32)
    tiles_e = (counts + tr - 1) // tr
    ends = jnp.cumsum(tiles_e)
    row0 = (ends - tiles_e) * tr
    expert = meta[:, :TOP_K].astype(jnp.int32)
    rank = meta[:, 2 * TOP_K:3 * TOP_K].astype(jnp.int32)
    pos_blk = (row0[expert] + rank).reshape(t // tm, 1, TOP_K * tm)
    n_used = ends[-1:].astype(jnp.int32)
    g = jnp.minimum(jnp.arange(n_tiles, dtype=jnp.int32), n_used - 1)
    tile_expert = jnp.sum(g[:, None] >= ends[None, :], axis=1).astype(jnp.int32)
    pad0 = (row0 + counts).astype(jnp.int32)
    padn = (tiles_e * tr - counts).astype(jnp.int32)

    xs = _scatter(pad0, padn, n_used, pos_blk, u3, n_tiles=n_tiles, tr=tr, tm=tm)
    ys = _experts(tile_expert, n_used, xs, wg, wu, wd, tr=tr)
    return _combine(pos_blk, x1, meta, p2d, gp, wpg, wpp, ys, tm=tm)


def _pad_lanes(v, offset=0):
    out = jnp.zeros((1, LANES), F32)
    return out.at[0, offset:offset + v.shape[0]].set(v.astype(F32))


def kernel(x, p, norm1_g, w_in, conv_w, conv_b, dt_bias, a_log, d_skip, ssd_norm_g, fg_bias, q_norm_g, k_norm_g, attn_norm_g, w_out, norm2_g, w_gate_dense, w_up_dense, w_down_dense, w_router, w_gate_exp, w_up_exp, w_down_exp, ple_norm_g, w_ple_gate, w_ple_proj):
    batch, seq, d_model = x.shape
    depth = w_in.shape[0]
    t = batch * seq
    d_ssd = N_HEADS * HEAD_DIM
    d_att = N_HEADS * HEAD_DIM
    d_conv = conv_w.shape[2]
    n_experts = w_router.shape[2]
    o_xbc = d_ssd
    o_dt = o_xbc + d_conv
    o_q = o_dt + N_HEADS
    o_k = o_q + d_att
    o_v = o_k + d_att
    o_f = o_v + d_att
    q_blk = (d_conv + d_ssd) // d_att

    ones_bd = (jnp.arange(d_att)[:, None] // HEAD_DIM
               == jnp.arange(d_att)[None, :] // HEAD_DIM).astype(BF16)
    row = lambda v: v.astype(F32).reshape(1, -1)

    xf = x.reshape(t, d_model)
    for i in range(depth):
        w = w_in[i]
        w_main = jnp.concatenate(
            [w[:, o_xbc:o_dt], w[:, :d_ssd], w[:, o_q:o_k], w[:, o_k:o_v], w[:, o_v:o_f]],
            axis=1).astype(BF16)
        w_small = jnp.zeros((d_model, LANES), F32)
        w_small = w_small.at[:, :N_HEADS].set(w[:, o_dt:o_q])
        w_small = w_small.at[:, N_HEADS:2 * N_HEADS].set(w[:, o_f:o_f + N_HEADS]).astype(BF16)

        main, small = _in_proj(xf, row(norm1_g[i]), w_main, w_small)

        y_ssd = _ssd(main, small, conv_w[i].astype(F32), row(conv_b[i]),
                     _pad_lanes(dt_bias[i]), _pad_lanes(a_log[i]),
                     row(jnp.repeat(d_skip[i], HEAD_DIM)), row(ssd_norm_g[i]),
                     batch=batch, seq=seq)

        gq = row(jnp.tile(q_norm_g[i], N_HEADS)) * (LOG2E * HEAD_DIM ** -0.5)
        gk = row(jnp.tile(k_norm_g[i], N_HEADS))
        qa, ka, va = _attn_prep(main, small, _pad_lanes(fg_bias[i], N_HEADS), gq, gk, ones_bd,
                                batch=batch, seq=seq, q_blk=q_blk)
        y_att = _flash(qa, ka, va, batch=batch, seq=seq)

        wo = w_out[i].astype(BF16)
        mix = (xf, y_ssd, y_att, p[i].reshape(t, -1), row(attn_norm_g[i]), wo[:d_ssd], wo[d_ssd:],
               row(norm2_g[i]))
        ple = (row(ple_norm_g[i]), w_ple_gate[i].astype(BF16), w_ple_proj[i].astype(BF16))
        j = i // 2
        if i % 2 == 0:
            xf = _dense_layer(*mix, w_gate_dense[j].astype(BF16), w_up_dense[j].astype(BF16),
                              w_down_dense[j].astype(BF16), *ple)
        else:
            wr = jnp.zeros((d_model, LANES), F32).at[:, :n_experts].set(w_router[j])
            wr_hi = wr.astype(BF16)
            wr_lo = (wr - wr_hi.astype(F32)).astype(BF16)
            xf = _moe_layer(*mix, wr_hi, wr_lo, w_gate_exp[j].astype(BF16),
                            w_up_exp[j].astype(BF16), w_down_exp[j].astype(BF16), *ple)
    return xf.reshape(batch, seq, d_model)
```

```python
import functools

import jax
import jax.numpy as jnp
from jax import lax
from jax.experimental import pallas as pl
from jax.experimental.pallas import tpu as pltpu

F32 = jnp.float32
BF16 = jnp.bfloat16

EPS = 1e-6
LOG2E = 1.4426950408889634
HEAD_DIM = 64
N_HEADS = 8
SSD_GROUPS = 2
D_STATE = 128
CONV_K = 4
CHUNK = 128
TOP_K = 2
LANES = 128
SUBLANES = 8
VMEM_LIMIT = 52 * 1024 * 1024


def _params(*sem):
    return pltpu.CompilerParams(dimension_semantics=sem, vmem_limit_bytes=VMEM_LIMIT)


def _rms(x, g):
    return x * lax.rsqrt(jnp.mean(x * x, axis=-1, keepdims=True) + EPS) * g


def _silu(x):
    return x / (1.0 + jnp.exp(-x))


def _softplus(x):
    return jnp.maximum(x, 0.0) + jnp.log(1.0 + jnp.exp(-jnp.abs(x)))


def _cumsum_rows(x):
    n = x.shape[0]
    row = lax.broadcasted_iota(jnp.int32, x.shape, 0)
    s = 1
    while s < n:
        x = x + jnp.where(row >= s, pltpu.roll(x, s, 0), 0.0)
        s *= 2
    return x


def _expand_heads(v, n_rows):
    lane = lax.broadcasted_iota(jnp.int32, (n_rows, LANES), 1)
    blocks = []
    for j in range(N_HEADS // 2):
        a = jnp.broadcast_to(v[:, 2 * j:2 * j + 1], (n_rows, LANES))
        b = jnp.broadcast_to(v[:, 2 * j + 1:2 * j + 2], (n_rows, LANES))
        blocks.append(jnp.where(lane < HEAD_DIM, a, b))
    return jnp.concatenate(blocks, axis=1)


def _in_proj_kernel(x_ref, g_ref, wm_ref, ws_ref, fgb_ref, gq_ref, gk_ref, ones_ref,
                    main_ref, small_ref, qa_ref, ka_ref, va_ref, carry_ref, *, tn):
    tm = x_ref.shape[0]
    d_att = N_HEADS * HEAD_DIM
    u = _rms(x_ref[...], g_ref[...]).astype(BF16)
    n_ssd = main_ref.shape[1]
    q, k, v = (jnp.dot(u, wm_ref[:, n_ssd + c * d_att:n_ssd + (c + 1) * d_att],
                       preferred_element_type=F32) for c in range(3))
    sm = jnp.dot(u, ws_ref[...], preferred_element_type=F32)
    small_ref[...] = sm
    _attn_rows(q, k, v, sm, fgb_ref, gq_ref, gk_ref, ones_ref, qa_ref, ka_ref, va_ref,
               carry_ref, tm)
    for n0 in range(0, n_ssd, tn):
        main_ref[:, n0:n0 + tn] = jnp.dot(
            u, wm_ref[:, n0:n0 + tn], preferred_element_type=F32).astype(BF16)


def _in_proj(x2d, g, w_main, w_small, fgb, gq, gk, ones_bd, *, batch, seq, tm=512, tn=512):
    t, d = x2d.shape
    nj = seq // tm
    d_att = N_HEADS * HEAD_DIM
    n_ssd = w_main.shape[1] - 3 * d_att
    const = lambda a: pl.BlockSpec(a.shape, lambda b, j: (0, 0))
    rows = lambda n: pl.BlockSpec((tm, n), lambda b, j: (b * nj + j, 0))
    aug = jax.ShapeDtypeStruct((batch, N_HEADS, seq, LANES), BF16)
    aug_spec = pl.BlockSpec((1, N_HEADS, tm, LANES), lambda b, j: (b, 0, j, 0))
    return pl.pallas_call(
        functools.partial(_in_proj_kernel, tn=tn),
        out_shape=(jax.ShapeDtypeStruct((t, n_ssd), BF16),
                   jax.ShapeDtypeStruct((t, LANES), F32), aug, aug, aug),
        grid=(batch, nj),
        in_specs=[rows(d), const(g), const(w_main), const(w_small), const(fgb), const(gq),
                  const(gk), const(ones_bd)],
        out_specs=(rows(n_ssd), rows(LANES), aug_spec, aug_spec, aug_spec),
        scratch_shapes=[pltpu.VMEM((1, LANES), F32)],
        compiler_params=_params("parallel", "arbitrary"),
        name="in_proj",
    )(x2d, g, w_main, w_small, fgb, gq, gk, ones_bd)


def _ssd_kernel(xbc_ref, z_ref, sm_ref, cw_ref, cb_ref, dtb_ref, alog_ref, dskip_ref, g_ref,
                y_ref, state_ref, cbuf_ref, *, lb):
    d_ssd = N_HEADS * HEAD_DIM
    gw = d_ssd // SSD_GROUPS

    @pl.when(pl.program_id(1) == 0)
    def _():
        state_ref[...] = jnp.zeros_like(state_ref)
        cbuf_ref[0:SUBLANES, :] = jnp.zeros((SUBLANES, cbuf_ref.shape[1]), F32)

    xbc = xbc_ref[...].astype(F32)
    cbuf_ref[SUBLANES:, :] = xbc
    cw = cw_ref[...]
    conv = cb_ref[...] + cw[CONV_K - 1:CONV_K, :] * xbc
    for k in range(CONV_K - 1):
        off = SUBLANES - (CONV_K - 1) + k
        conv = conv + cw[k:k + 1, :] * cbuf_ref[off:off + lb, :]
    cbuf_ref[0:SUBLANES, :] = xbc[lb - SUBLANES:, :]
    act = _silu(conv)

    a_neg = -jnp.exp(alog_ref[...])
    dtb = dtb_ref[...]
    dskip = dskip_ref[...]
    gain = g_ref[...]
    tri = (lax.broadcasted_iota(jnp.int32, (CHUNK, CHUNK), 0)
           >= lax.broadcasted_iota(jnp.int32, (CHUNK, CHUNK), 1))
    lane = lax.broadcasted_iota(jnp.int32, (CHUNK, LANES), 1)

    for c in range(lb // CHUNK):
        r0 = c * CHUNK
        xs = act[r0:r0 + CHUNK, 0:d_ssd]
        bm = act[r0:r0 + CHUNK, d_ssd:d_ssd + SSD_GROUPS * D_STATE].astype(BF16)
        cm = act[r0:r0 + CHUNK, d_ssd + SSD_GROUPS * D_STATE:].astype(BF16)
        dt = _softplus(sm_ref[r0:r0 + CHUNK, :] + dtb)
        acum = _cumsum_rows(dt * a_neg)
        acum_t = acum.T
        total = acum[CHUNK - 1:CHUNK, :]

        dt_e = _expand_heads(dt, CHUNK)
        in_decay_e = _expand_heads(jnp.exp(acum), CHUNK)
        out_decay_e = _expand_heads(jnp.exp(total - acum), CHUNK)
        chunk_decay_e = _expand_heads(jnp.exp(total), 1)

        xdt = xs * dt_e
        xdt_b = xdt.astype(BF16)
        w_state = (xdt * out_decay_e).astype(BF16)

        y_parts = []
        for g in range(SSD_GROUPS):
            bg = bm[:, g * D_STATE:(g + 1) * D_STATE]
            cg = cm[:, g * D_STATE:(g + 1) * D_STATE]
            cb = lax.dot_general(cg, bg, (((1,), (1,)), ((), ())),
                                 preferred_element_type=F32)
            st = state_ref[g]
            y_off = jnp.dot(cg, st.astype(BF16), preferred_element_type=F32)
            for pr in range(gw // LANES):
                pair = []
                col0 = g * gw + pr * LANES
                for hh in range(2):
                    h = (col0 // HEAD_DIM) + hh
                    seg = acum[:, h:h + 1] - acum_t[h:h + 1, :]
                    m = (cb * jnp.where(tri, jnp.exp(seg), 0.0)).astype(BF16)
                    pair.append(jnp.dot(m, xdt_b[:, col0:col0 + LANES],
                                        preferred_element_type=F32))
                y_parts.append(jnp.where(lane < HEAD_DIM, pair[0], pair[1])
                               + y_off[:, pr * LANES:(pr + 1) * LANES]
                               * in_decay_e[:, col0:col0 + LANES])
            upd = lax.dot_general(bg, w_state[:, g * gw:(g + 1) * gw], (((0,), (0,)), ((), ())),
                                  preferred_element_type=F32)
            state_ref[g] = st * chunk_decay_e[:, g * gw:(g + 1) * gw] + upd

        y = jnp.concatenate(y_parts, axis=1) + dskip * xs
        y = y * _silu(z_ref[r0:r0 + CHUNK, :].astype(F32))
        y_ref[r0:r0 + CHUNK, :] = _rms(y, gain).astype(BF16)


def _ssd(main, small, conv_w, conv_b, dtb, alog, dskip_e, gain, *, batch, seq, lb=512):
    t = batch * seq
    nj = seq // lb
    d_ssd = N_HEADS * HEAD_DIM
    d_conv = conv_w.shape[1]
    z_blk = d_conv // d_ssd
    vec = lambda n: pl.BlockSpec((1, n), lambda b, j: (0, 0))
    return pl.pallas_call(
        functools.partial(_ssd_kernel, lb=lb),
        out_shape=jax.ShapeDtypeStruct((t, d_ssd), BF16),
        grid=(batch, nj),
        in_specs=[pl.BlockSpec((lb, d_conv), lambda b, j: (b * nj + j, 0)),
                  pl.BlockSpec((lb, d_ssd), lambda b, j: (b * nj + j, z_blk)),
                  pl.BlockSpec((lb, LANES), lambda b, j: (b * nj + j, 0)),
                  pl.BlockSpec((CONV_K, d_conv), lambda b, j: (0, 0)),
                  vec(d_conv), vec(LANES), vec(LANES), vec(d_ssd), vec(d_ssd)],
        out_specs=pl.BlockSpec((lb, d_ssd), lambda b, j: (b * nj + j, 0)),
        scratch_shapes=[pltpu.VMEM((SSD_GROUPS, D_STATE, d_ssd // SSD_GROUPS), F32),
                        pltpu.VMEM((SUBLANES + lb, d_conv), F32)],
        compiler_params=_params("parallel", "arbitrary"),
        name="ssd",
    )(main, main, small, conv_w, conv_b, dtb, alog, dskip_e, gain)


def _attn_rows(q, k, vv, sm, fgb_ref, gq_ref, gk_ref, ones_ref, qa_ref, ka_ref, va_ref,
               carry_ref, ts):
    @pl.when(pl.program_id(1) == 0)
    def _():
        carry_ref[...] = jnp.zeros_like(carry_ref)

    def headnorm(v, g):
        ssq = jnp.dot((v * v).astype(BF16), ones_ref[...], preferred_element_type=F32)
        return v * lax.rsqrt(ssq * (1.0 / HEAD_DIM) + EPS) * g

    qn = headnorm(q, gq_ref[...])
    kn = headnorm(k, gk_ref[...])

    xf = sm + fgb_ref[...]
    log_f = jnp.minimum(xf, 0.0) - jnp.log(1.0 + jnp.exp(-jnp.abs(xf)))
    fcum = _cumsum_rows(log_f) + carry_ref[...]
    carry_ref[...] = fcum[ts - 1:ts, :]
    f2 = fcum * LOG2E
    f_hi = f2.astype(BF16).astype(F32)
    rem = f2 - f_hi
    f_mid = rem.astype(BF16).astype(F32)
    f_lo = rem - f_mid

    lane = lax.broadcasted_iota(jnp.int32, (ts, LANES), 1)
    d = HEAD_DIM
    v_tail = jnp.where(lane == d, 1.0, 0.0)
    for h in range(N_HEADS):
        j = h // 2
        qb = qn[:, j * LANES:(j + 1) * LANES]
        kb = kn[:, j * LANES:(j + 1) * LANES]
        vb = vv[:, j * LANES:(j + 1) * LANES]
        if h % 2:
            qb = pltpu.roll(qb, HEAD_DIM, 1)
            kb = pltpu.roll(kb, HEAD_DIM, 1)
            vb = pltpu.roll(vb, HEAD_DIM, 1)
        va_ref[0, h] = jnp.where(lane < d, vb, v_tail).astype(BF16)
        fl = N_HEADS + h
        bh = jnp.broadcast_to(f_hi[:, fl:fl + 1], (ts, LANES))
        bm = jnp.broadcast_to(f_mid[:, fl:fl + 1], (ts, LANES))
        bl = jnp.broadcast_to(f_lo[:, fl:fl + 1], (ts, LANES))
        qx = jnp.where(lane == d, bh, jnp.where(lane == d + 1, bm, jnp.where(
            lane == d + 2, bl, jnp.where(lane < d + 6, 1.0, 0.0))))
        kx = jnp.where(lane < d + 3, 1.0, jnp.where(lane == d + 3, -bh, jnp.where(
            lane == d + 4, -bm, jnp.where(lane == d + 5, -bl, 0.0))))
        qa_ref[0, h] = jnp.where(lane < d, qb, qx).astype(BF16)
        ka_ref[0, h] = jnp.where(lane < d, kb, kx).astype(BF16)


def _flash_kernel(qa_ref, ka_ref, va_ref, o_ref, s_ref, m_ref, acc_ref, *, bq):
    i = pl.program_id(2)
    lane = lax.broadcasted_iota(jnp.int32, (bq, LANES), 1)
    tri = (lax.broadcasted_iota(jnp.int32, (bq, bq), 0)
           >= lax.broadcasted_iota(jnp.int32, (bq, bq), 1))

    def logits(kb, slot):
        k0 = pl.multiple_of(kb * bq, bq)
        for hh in range(2):
            s_ref[slot, hh] = lax.dot_general(
                qa_ref[0, hh], ka_ref[0, hh, pl.ds(k0, bq), :], (((1,), (1,)), ((), ())),
                preferred_element_type=F32)

    def consume(kb, slot, masked):
        k0 = pl.multiple_of(kb * bq, bq)
        for hh in range(2):
            s = s_ref[slot, hh]
            if masked:
                s = jnp.where(tri, s, -jnp.inf)
            m = m_ref[hh]
            m_new = jnp.maximum(m, jnp.max(s, axis=-1, keepdims=True))
            p = jnp.exp2((s - m_new).astype(BF16))
            acc_ref[hh] = (jnp.exp2(m - m_new) * acc_ref[hh]
                           + jnp.dot(p, va_ref[0, hh, pl.ds(k0, bq), :],
                                     preferred_element_type=F32))
            m_ref[hh] = m_new

    m_ref[...] = jnp.full(m_ref.shape, -jnp.inf, F32)
    acc_ref[...] = jnp.zeros(acc_ref.shape, F32)
    logits(0, 0)

    def pair(j, carry):
        kb = 2 * j
        logits(kb + 1, 1)
        consume(kb, 0, False)
        logits(kb + 2, 0)
        consume(kb + 1, 1, False)
        return carry

    lax.fori_loop(0, i // 2, pair, 0)

    @pl.when(i % 2 == 0)
    def _():
        consume(i, 0, True)

    @pl.when(i % 2 == 1)
    def _():
        logits(i, 1)
        consume(i - 1, 0, False)
        consume(i, 1, True)

    acc0 = acc_ref[0]
    acc1 = acc_ref[1]
    o0 = acc0 / acc0[:, HEAD_DIM:HEAD_DIM + 1]
    o1 = acc1 / acc1[:, HEAD_DIM:HEAD_DIM + 1]
    o_ref[...] = jnp.where(lane < HEAD_DIM, o0, pltpu.roll(o1, HEAD_DIM, 1)).astype(BF16)


def _flash(qa, ka, va, *, batch, seq, bq=512):
    nq = seq // bq
    npair = N_HEADS // 2
    whole = pl.BlockSpec((1, 2, seq, LANES), lambda b, hp, i: (b, hp, 0, 0))
    return pl.pallas_call(
        functools.partial(_flash_kernel, bq=bq),
        out_shape=jax.ShapeDtypeStruct((batch * seq, N_HEADS * HEAD_DIM), BF16),
        grid=(batch, npair, nq),
        in_specs=[pl.BlockSpec((1, 2, bq, LANES), lambda b, hp, i: (b, hp, i, 0)), whole, whole],
        out_specs=pl.BlockSpec((bq, LANES), lambda b, hp, i: (b * nq + i, hp)),
        scratch_shapes=[pltpu.VMEM((2, 2, bq, bq), F32),
                        pltpu.VMEM((2, bq, 1), F32),
                        pltpu.VMEM((2, bq, LANES), F32)],
        compiler_params=_params("parallel", "parallel", "arbitrary"),
        name="flash",
    )(qa, ka, va)


def _mix_residual(x_ref, ys_ref, ya_ref, ga_ref, wt_ref, wb_ref):
    ya = _rms(ya_ref[...].astype(F32), ga_ref[...]).astype(BF16)
    return (x_ref[...]
            + jnp.dot(ys_ref[...], wt_ref[...], preferred_element_type=F32)
            + jnp.dot(ya, wb_ref[...], preferred_element_type=F32))


def _ple_add(x, p_ref, g_ref, wg_ref, wp_ref):
    u = _rms(x, g_ref[...]).astype(BF16)
    gate = 1.0 / (1.0 + jnp.exp(-jnp.dot(u, wg_ref[...], preferred_element_type=F32)))
    pe = jnp.dot(p_ref[...].astype(BF16), wp_ref[...], preferred_element_type=F32)
    return x + gate * pe


def _const_spec(a):
    nd = a.ndim
    return pl.BlockSpec(a.shape, lambda *_: (0,) * nd, pipeline_mode=pl.Buffered(1))


def _split_rows(ref, x):
    n = x.shape[0]
    for s in range(SUBLANES):
        ref[pl.ds(s, n, stride=SUBLANES), :] = x[:, s * LANES:(s + 1) * LANES]


def _merge_rows(ref):
    n = ref.shape[0]
    flat = ref.reshape(n * SUBLANES, LANES)
    return jnp.concatenate([flat[pl.ds(s, n, stride=SUBLANES), :] for s in range(SUBLANES)], axis=1)


def _ffn_chunks(d_ff, step):
    return [(f0, min(step, d_ff - f0)) for f0 in range(0, d_ff, step)]


def _swiglu(u, wg, wu, wd, fc):
    y = None
    for f0, fw in _ffn_chunks(wg.shape[1], fc):
        hg = jnp.dot(u, wg[:, f0:f0 + fw], preferred_element_type=F32)
        hu = jnp.dot(u, wu[:, f0:f0 + fw], preferred_element_type=F32)
        h = (_silu(hg) * hu).astype(BF16)
        part = jnp.dot(h, wd[f0:f0 + fw, :], preferred_element_type=F32)
        y = part if y is None else y + part
    return y


def _dense_layer_kernel(x_ref, ys_ref, ya_ref, p_ref, ga_ref, wt_ref, wb_ref, g2_ref,
                        wg_ref, wu_ref, wd_ref, gp_ref, wpg_ref, wpp_ref, o_ref, *, fc):
    x1 = _mix_residual(x_ref, ys_ref, ya_ref, ga_ref, wt_ref, wb_ref)
    u = _rms(x1, g2_ref[...]).astype(BF16)
    x2 = x1 + _swiglu(u, wg_ref, wu_ref, wd_ref, fc)
    o_ref[...] = _ple_add(x2, p_ref, gp_ref, wpg_ref, wpp_ref)


def _dense_layer(x2d, y_ssd, y_att, p2d, ga, w_top, w_bot, g2, wg, wu, wd, gp, wpg, wpp,
                 *, tm=512, fc=1024):
    t, d = x2d.shape
    row = lambda a: pl.BlockSpec((tm, a.shape[1]), lambda i: (i, 0))
    consts = (ga, w_top, w_bot, g2, wg, wu, wd, gp, wpg, wpp)
    return pl.pallas_call(
        functools.partial(_dense_layer_kernel, fc=fc),
        out_shape=jax.ShapeDtypeStruct((t, d), F32),
        grid=(t // tm,),
        in_specs=[row(x2d), row(y_ssd), row(y_att), row(p2d)] + [_const_spec(a) for a in consts],
        out_specs=row(x2d),
        compiler_params=_params("parallel"),
        name="dense_layer",
    )(x2d, y_ssd, y_att, p2d, *consts)


def _route_kernel(x_ref, ys_ref, ya_ref, ga_ref, wt_ref, wb_ref, g2_ref, wrh_ref, wrl_ref,
                  x1_ref, u3_ref, meta_ref, mt_ref, cnt_ref, carry_ref, *, n_experts):
    tm = x_ref.shape[0]

    @pl.when(pl.program_id(0) == 0)
    def _():
        carry_ref[...] = jnp.zeros_like(carry_ref)

    x1 = _mix_residual(x_ref, ys_ref, ya_ref, ga_ref, wt_ref, wb_ref)
    x1_ref[...] = x1
    u = _rms(x1, g2_ref[...])
    _split_rows(u3_ref, u)

    u_hi = u.astype(BF16)
    u_lo = (u - u_hi.astype(F32)).astype(BF16)
    logits = (jnp.dot(u_hi, wrh_ref[...], preferred_element_type=F32)
              + jnp.dot(u_lo, wrh_ref[...], preferred_element_type=F32)
              + jnp.dot(u_hi, wrl_ref[...], preferred_element_type=F32))
    lane_i = lax.broadcasted_iota(jnp.int32, (tm, LANES), 1)
    lane = lane_i.astype(F32)
    logits = jnp.where(lane_i < n_experts, logits, -jnp.inf)
    m1 = jnp.max(logits, axis=-1, keepdims=True)
    i1 = jnp.min(jnp.where(logits == m1, lane, float(LANES)), axis=-1, keepdims=True)
    rest = jnp.where(lane == i1, -jnp.inf, logits)
    m2 = jnp.max(rest, axis=-1, keepdims=True)
    i2 = jnp.min(jnp.where(rest == m2, lane, float(LANES)), axis=-1, keepdims=True)
    e2 = jnp.exp(m2 - m1)
    g1 = 1.0 / (1.0 + e2)
    g2 = e2 * g1

    hit = jnp.where(lane == i1, 1.0, jnp.where(lane == i2, 1.0, 0.0))
    incl = _cumsum_rows(hit)
    before = incl - hit + carry_ref[...]
    r1 = jnp.sum(jnp.where(lane == i1, before, 0.0), axis=-1, keepdims=True)
    r2 = jnp.sum(jnp.where(lane == i2, before, 0.0), axis=-1, keepdims=True)
    meta = jnp.where(lane_i == 0, i1, jnp.where(lane_i == 1, i2, jnp.where(
        lane_i == 2, g1, jnp.where(lane_i == 3, g2, jnp.where(
            lane_i == 4, r1, jnp.where(lane_i == 5, r2, 0.0))))))
    meta_ref[...] = meta
    mt_ref[...] = meta.T[0:SUBLANES, :]
    total = carry_ref[...] + incl[tm - 1:tm, :]
    carry_ref[...] = total
    cnt_ref[...] = total


def _route(x2d, y_ssd, y_att, ga, w_top, w_bot, g2, wr_hi, wr_lo, *, n_experts, tm):
    t, d = x2d.shape
    row = lambda a: pl.BlockSpec((tm, a.shape[1]), lambda i: (i, 0))
    consts = (ga, w_top, w_bot, g2, wr_hi, wr_lo)
    return pl.pallas_call(
        functools.partial(_route_kernel, n_experts=n_experts),
        out_shape=(jax.ShapeDtypeStruct((t, d), F32),
                   jax.ShapeDtypeStruct((t * SUBLANES, LANES), F32),
                   jax.ShapeDtypeStruct((t, LANES), F32),
                   jax.ShapeDtypeStruct((t // tm * SUBLANES, tm), F32),
                   jax.ShapeDtypeStruct((1, LANES), F32)),
        grid=(t // tm,),
        in_specs=[row(x2d), row(y_ssd), row(y_att)] + [_const_spec(a) for a in consts],
        out_specs=(row(x2d),
                   pl.BlockSpec((tm * SUBLANES, LANES), lambda i: (i, 0)),
                   pl.BlockSpec((tm, LANES), lambda i: (i, 0)),
                   pl.BlockSpec((SUBLANES, tm), lambda i: (i, 0)),
                   pl.BlockSpec((1, LANES), lambda i: (0, 0))),
        scratch_shapes=[pltpu.VMEM((1, LANES), F32)],
        compiler_params=_params("arbitrary"),
        name="route",
    )(x2d, y_ssd, y_att, *consts)


def _scatter_kernel(pad0_ref, padn_ref, nu_ref, pos_ref, u3_ref, xs_ref, zero_ref, sem,
                    *, n_experts, tr):
    tm = u3_ref.shape[0]

    def row_copy(r, dst):
        return pltpu.make_async_copy(u3_ref.at[pl.ds(r, 1)], xs_ref.at[pl.ds(dst, 1)], sem)

    def issue(r, carry):
        for k in range(TOP_K):
            row_copy(r, pos_ref[0, 0, k * tm + r]).start(priority=k)
        return carry

    lax.fori_loop(0, tm, issue, 0, unroll=8)

    def drain(r, carry):
        for k in range(TOP_K):
            row_copy(r, 0).wait()
        return carry

    lax.fori_loop(0, tm, drain, 0, unroll=8)

    @pl.when(pl.program_id(0) == pl.num_programs(0) - 1)
    def _():
        zero_ref[...] = jnp.zeros_like(zero_ref)

        def zero_row(dst):
            return pltpu.make_async_copy(zero_ref.at[pl.ds(0, 1)], xs_ref.at[pl.ds(dst, 1)], sem)

        def zero_tile(g):
            return pltpu.make_async_copy(zero_ref, xs_ref.at[pl.ds(g * tr, tr)], sem)

        def loop(lo, hi, fn):
            lax.fori_loop(lo, hi, lambda j, c: (fn(j), c)[1], 0)

        for e in range(n_experts):
            loop(0, padn_ref[e], lambda j: zero_row(pad0_ref[e] + j).start())
            loop(0, padn_ref[e], lambda j: zero_row(0).wait())
        n_tiles = xs_ref.shape[0] // tr
        loop(nu_ref[0], n_tiles, lambda g: zero_tile(g).start())
        loop(nu_ref[0], n_tiles, lambda g: zero_tile(0).wait())


def _scatter(pad0, padn, n_used, pos_blk, u3, *, n_tiles, tr, tm):
    t = u3.shape[0]
    return pl.pallas_call(
        functools.partial(_scatter_kernel, n_experts=pad0.shape[0], tr=tr),
        out_shape=jax.ShapeDtypeStruct((n_tiles * tr, SUBLANES, LANES), F32),
        grid_spec=pltpu.PrefetchScalarGridSpec(
            num_scalar_prefetch=3,
            grid=(t // tm,),
            in_specs=[pl.BlockSpec((1, 1, TOP_K * tm), lambda i, *_: (i, 0, 0),
                                   memory_space=pltpu.SMEM),
                      pl.BlockSpec((tm, SUBLANES, LANES), lambda i, *_: (i, 0, 0))],
            out_specs=pl.BlockSpec(memory_space=pl.ANY),
            scratch_shapes=[pltpu.VMEM((tr, SUBLANES, LANES), F32), pltpu.SemaphoreType.DMA(())]),
        compiler_params=_params("arbitrary"),
        name="scatter_rows",
    )(pad0, padn, n_used, pos_blk, u3)


def _expert_kernel(te_ref, nu_ref, xs_ref, wg_ref, wu_ref, wd_ref, ys_ref, *, fc):
    used = pl.program_id(0) < nu_ref[0]

    @pl.when(used)
    def _():
        u = _merge_rows(xs_ref).astype(BF16)
        _split_rows(ys_ref, _swiglu(u, wg_ref.at[0], wu_ref.at[0], wd_ref.at[0], fc))

    @pl.when(jnp.logical_not(used))
    def _():
        ys_ref[...] = jnp.zeros_like(ys_ref)


def _experts(tile_expert, n_used, xs, wg, wu, wd, *, tr, fc=768):
    _, d, dff = wg.shape
    rows = pl.BlockSpec((tr * SUBLANES, LANES), lambda g, te, nu: (g, 0))
    return pl.pallas_call(
        functools.partial(_expert_kernel, fc=fc),
        out_shape=jax.ShapeDtypeStruct((xs.shape[0] * SUBLANES, LANES), F32),
        grid_spec=pltpu.PrefetchScalarGridSpec(
            num_scalar_prefetch=2,
            grid=(xs.shape[0] // tr,),
            in_specs=[pl.BlockSpec((tr, SUBLANES, LANES),
                                   lambda g, te, nu: (jnp.minimum(g, nu[0] - 1), 0, 0)),
                      pl.BlockSpec((1, d, dff), lambda g, te, nu: (te[g], 0, 0)),
                      pl.BlockSpec((1, d, dff), lambda g, te, nu: (te[g], 0, 0)),
                      pl.BlockSpec((1, dff, d), lambda g, te, nu: (te[g], 0, 0))],
            out_specs=rows),
        compiler_params=_params("arbitrary"),
        name="experts",
    )(tile_expert, n_used, xs, wg, wu, wd)


def _combine_kernel(pos_ref, nxt_ref, x1_ref, meta_ref, p_ref, gp_ref, wpg_ref, wpp_ref, ys_ref,
                    o_ref, ybuf_ref, sem):
    tm = x1_ref.shape[0]
    i = pl.program_id(0)
    slot = lax.rem(i, 2)

    def row_copy(sl, r, k, src):
        return pltpu.make_async_copy(ys_ref.at[pl.ds(src, 1)], ybuf_ref.at[sl, k, pl.ds(r, 1)],
                                     sem.at[sl])

    def request(rows_ref, sl):
        def issue(r, carry):
            for k in range(TOP_K):
                row_copy(sl, r, k, rows_ref[0, 0, k * tm + r]).start(priority=k)
            return carry
        lax.fori_loop(0, tm, issue, 0, unroll=8)

    @pl.when(i == 0)
    def _():
        request(pos_ref, 0)

    @pl.when(i + 1 < pl.num_programs(0))
    def _():
        request(nxt_ref, 1 - slot)

    def drain(r, carry):
        for k in range(TOP_K):
            row_copy(slot, r, k, 0).wait()
        return carry

    lax.fori_loop(0, tm, drain, 0, unroll=8)

    meta = meta_ref[...]
    x2 = (x1_ref[...] + meta[:, 2:3] * _merge_rows(ybuf_ref.at[slot, 0])
          + meta[:, 3:4] * _merge_rows(ybuf_ref.at[slot, 1]))
    o_ref[...] = _ple_add(x2, p_ref, gp_ref, wpg_ref, wpp_ref)


def _combine(pos_blk, x1, meta, p2d, gp, wpg, wpp, ys, *, tm):
    t, d = x1.shape
    nt = t // tm
    row = lambda a: pl.BlockSpec((tm, a.shape[1]), lambda i: (i, 0))
    rows_of = lambda f: pl.BlockSpec((1, 1, TOP_K * tm), lambda i: (f(i), 0, 0),
                                     memory_space=pltpu.SMEM)
    return pl.pallas_call(
        _combine_kernel,
        out_shape=jax.ShapeDtypeStruct((t, d), F32),
        grid=(nt,),
        in_specs=[rows_of(lambda i: i), rows_of(lambda i: jnp.minimum(i + 1, nt - 1)),
                  row(x1), row(meta), row(p2d), _const_spec(gp), _const_spec(wpg),
                  _const_spec(wpp), pl.BlockSpec(memory_space=pl.ANY)],
        out_specs=row(x1),
        scratch_shapes=[pltpu.VMEM((2, TOP_K, tm, SUBLANES, LANES), F32),
                        pltpu.SemaphoreType.DMA((2,))],
        compiler_params=_params("arbitrary"),
        name="combine_ple",
    )(pos_blk, pos_blk, x1, meta, p2d, gp, wpg, wpp, ys)


def _moe_layer(x2d, y_ssd, y_att, p2d, ga, w_top, w_bot, g2, wr_hi, wr_lo, wg, wu, wd,
               gp, wpg, wpp, *, tm=512, tr=512):
    t = x2d.shape[0]
    n_experts = wg.shape[0]
    x1, u3, meta, meta_t, cnt = _route(x2d, y_ssd, y_att, ga, w_top, w_bot, g2, wr_hi, wr_lo,
                               n_experts=n_experts, tm=tm)

    n_tiles = t * TOP_K // tr + n_experts
    counts = cnt[0, :n_experts].astype(jnp.int32)
    tiles_e = (counts + tr - 1) // tr
    ends = jnp.cumsum(tiles_e)
    row0 = (ends - tiles_e) * tr
    meta_t = meta_t.reshape(t // tm, SUBLANES, tm)
    expert = meta_t[:, :TOP_K].astype(jnp.int32)
    rank = meta_t[:, 2 * TOP_K:3 * TOP_K].astype(jnp.int32)
    pos_blk = (row0[expert] + rank).reshape(t // tm, 1, TOP_K * tm)
    n_used = ends[-1:].astype(jnp.int32)
    g = jnp.minimum(jnp.arange(n_tiles, dtype=jnp.int32), n_used - 1)
    tile_expert = jnp.sum(g[:, None] >= ends[None, :], axis=1).astype(jnp.int32)
    pad0 = (row0 + counts).astype(jnp.int32)
    padn = (tiles_e * tr - counts).astype(jnp.int32)

    as_tiles = lambda a: a.reshape(-1, SUBLANES, LANES)
    xs = _scatter(pad0, padn, n_used, pos_blk, as_tiles(u3), n_tiles=n_tiles, tr=tr, tm=tm)
    ys = _experts(tile_expert, n_used, xs, wg, wu, wd, tr=tr)
    return _combine(pos_blk, x1, meta, p2d, gp, wpg, wpp, as_tiles(ys), tm=tm)


def _pad_lanes(v, offset=0):
    out = jnp.zeros((1, LANES), F32)
    return out.at[0, offset:offset + v.shape[0]].set(v.astype(F32))


def kernel(x, p, norm1_g, w_in, conv_w, conv_b, dt_bias, a_log, d_skip, ssd_norm_g, fg_bias, q_norm_g, k_norm_g, attn_norm_g, w_out, norm2_g, w_gate_dense, w_up_dense, w_down_dense, w_router, w_gate_exp, w_up_exp, w_down_exp, ple_norm_g, w_ple_gate, w_ple_proj):
    batch, seq, d_model = x.shape
    depth = w_in.shape[0]
    t = batch * seq
    d_ssd = N_HEADS * HEAD_DIM
    d_att = N_HEADS * HEAD_DIM
    d_conv = conv_w.shape[2]
    n_experts = w_router.shape[2]
    o_xbc = d_ssd
    o_dt = o_xbc + d_conv
    o_q = o_dt + N_HEADS
    o_k = o_q + d_att
    o_v = o_k + d_att
    o_f = o_v + d_att

    ones_bd = (jnp.arange(d_att)[:, None] // HEAD_DIM
               == jnp.arange(d_att)[None, :] // HEAD_DIM).astype(BF16)
    row = lambda v: v.astype(F32).reshape(1, -1)

    xf = x.reshape(t, d_model)
    for i in range(depth):
        w = w_in[i]
        w_main = jnp.concatenate(
            [w[:, o_xbc:o_dt], w[:, :d_ssd], w[:, o_q:o_k], w[:, o_k:o_v], w[:, o_v:o_f]],
            axis=1).astype(BF16)
        w_small = jnp.zeros((d_model, LANES), F32)
        w_small = w_small.at[:, :N_HEADS].set(w[:, o_dt:o_q])
        w_small = w_small.at[:, N_HEADS:2 * N_HEADS].set(w[:, o_f:o_f + N_HEADS]).astype(BF16)

        gq = row(jnp.tile(q_norm_g[i], N_HEADS)) * (LOG2E * HEAD_DIM ** -0.5)
        gk = row(jnp.tile(k_norm_g[i], N_HEADS))
        main, small, qa, ka, va = _in_proj(
            xf, row(norm1_g[i]), w_main, w_small, _pad_lanes(fg_bias[i], N_HEADS), gq, gk,
            ones_bd, batch=batch, seq=seq)

        y_ssd = _ssd(main, small, conv_w[i].astype(F32), row(conv_b[i]),
                     _pad_lanes(dt_bias[i]), _pad_lanes(a_log[i]),
                     row(jnp.repeat(d_skip[i], HEAD_DIM)), row(ssd_norm_g[i]),
                     batch=batch, seq=seq)
        y_att = _flash(qa, ka, va, batch=batch, seq=seq)

        wo = w_out[i].astype(BF16)
        mix = (xf, y_ssd, y_att, p[i].reshape(t, -1), row(attn_norm_g[i]), wo[:d_ssd], wo[d_ssd:],
               row(norm2_g[i]))
        ple = (row(ple_norm_g[i]), w_ple_gate[i].astype(BF16), w_ple_proj[i].astype(BF16))
        j = i // 2
        if i % 2 == 0:
            xf = _dense_layer(*mix, w_gate_dense[j].astype(BF16), w_up_dense[j].astype(BF16),
                              w_down_dense[j].astype(BF16), *ple)
        else:
            wr = jnp.zeros((d_model, LANES), F32).at[:, :n_experts].set(w_router[j])
            wr_hi = wr.astype(BF16)
            wr_lo = (wr - wr_hi.astype(F32)).astype(BF16)
            xf = _moe_layer(*mix, wr_hi, wr_lo, w_gate_exp[j].astype(BF16),
                            w_up_exp[j].astype(BF16), w_down_exp[j].astype(BF16), *ple)
    return xf.reshape(batch, seq, d_model)
```

```python
import functools

import jax
import jax.numpy as jnp
from jax import lax
from jax.experimental import pallas as pl
from jax.experimental.pallas import tpu as pltpu

F32 = jnp.float32
BF16 = jnp.bfloat16

EPS = 1e-6
LOG2E = 1.4426950408889634
HEAD_DIM = 64
N_HEADS = 8
SSD_GROUPS = 2
D_STATE = 128
CONV_K = 4
CHUNK = 128
TOP_K = 2
LANES = 128
SUBLANES = 8
VMEM_LIMIT = 52 * 1024 * 1024


def _params(*sem):
    return pltpu.CompilerParams(dimension_semantics=sem, vmem_limit_bytes=VMEM_LIMIT)


def _rms(x, g):
    return x * lax.rsqrt(jnp.mean(x * x, axis=-1, keepdims=True) + EPS) * g


def _silu(x):
    return x / (1.0 + jnp.exp(-x))


def _softplus(x):
    return jnp.maximum(x, 0.0) + jnp.log(1.0 + jnp.exp(-jnp.abs(x)))


def _cumsum_rows(x):
    n = x.shape[0]
    row = lax.broadcasted_iota(jnp.int32, x.shape, 0)
    s = 1
    while s < n:
        x = x + jnp.where(row >= s, pltpu.roll(x, s, 0), 0.0)
        s *= 2
    return x


def _expand_heads(v, n_rows):
    lane = lax.broadcasted_iota(jnp.int32, (n_rows, LANES), 1)
    blocks = []
    for j in range(N_HEADS // 2):
        a = jnp.broadcast_to(v[:, 2 * j:2 * j + 1], (n_rows, LANES))
        b = jnp.broadcast_to(v[:, 2 * j + 1:2 * j + 2], (n_rows, LANES))
        blocks.append(jnp.where(lane < HEAD_DIM, a, b))
    return jnp.concatenate(blocks, axis=1)


def _in_proj_kernel(x_ref, g_ref, wm_ref, ws_ref, fgb_ref, gq_ref, gk_ref, ones_ref,
                    main_ref, small_ref, qa_ref, ka_ref, va_ref, carry_ref, *, tn):
    tm = x_ref.shape[0]
    d_att = N_HEADS * HEAD_DIM
    u = _rms(x_ref[...], g_ref[...]).astype(BF16)
    n_ssd = main_ref.shape[1]
    def project(rows):
        ug = u[rows, :]
        q, k, v = (jnp.dot(ug, wm_ref[:, n_ssd + c * d_att:n_ssd + (c + 1) * d_att],
                           preferred_element_type=F32) for c in range(3))
        sm = jnp.dot(ug, ws_ref[...], preferred_element_type=F32)
        small_ref[rows, :] = sm
        return q, k, v, sm

    _attn_rows(project, fgb_ref, gq_ref, gk_ref, ones_ref, qa_ref, ka_ref, va_ref,
               carry_ref, tm)
    for n0 in range(0, n_ssd, tn):
        main_ref[:, n0:n0 + tn] = jnp.dot(
            u, wm_ref[:, n0:n0 + tn], preferred_element_type=F32).astype(BF16)


def _in_proj(x2d, g, w_main, w_small, fgb, gq, gk, ones_bd, *, batch, seq, tm=512, tn=512):
    t, d = x2d.shape
    nj = seq // tm
    d_att = N_HEADS * HEAD_DIM
    n_ssd = w_main.shape[1] - 3 * d_att
    const = lambda a: pl.BlockSpec(a.shape, lambda b, j: (0, 0))
    rows = lambda n: pl.BlockSpec((tm, n), lambda b, j: (b * nj + j, 0))
    aug = jax.ShapeDtypeStruct((batch, N_HEADS, seq, LANES), BF16)
    aug_spec = pl.BlockSpec((1, N_HEADS, tm, LANES), lambda b, j: (b, 0, j, 0))
    return pl.pallas_call(
        functools.partial(_in_proj_kernel, tn=tn),
        out_shape=(jax.ShapeDtypeStruct((t, n_ssd), BF16),
                   jax.ShapeDtypeStruct((t, LANES), F32), aug, aug, aug),
        grid=(batch, nj),
        in_specs=[rows(d), const(g), const(w_main), const(w_small), const(fgb), const(gq),
                  const(gk), const(ones_bd)],
        out_specs=(rows(n_ssd), rows(LANES), aug_spec, aug_spec, aug_spec),
        scratch_shapes=[pltpu.VMEM((1, LANES), F32)],
        compiler_params=_params("parallel", "arbitrary"),
        name="in_proj",
    )(x2d, g, w_main, w_small, fgb, gq, gk, ones_bd)


def _ssd_kernel(xbc_ref, z_ref, sm_ref, cw_ref, cb_ref, dtb_ref, alog_ref, dskip_ref, g_ref,
                y_ref, state_ref, cbuf_ref, *, lb):
    d_ssd = N_HEADS * HEAD_DIM
    gw = d_ssd // SSD_GROUPS

    @pl.when(pl.program_id(1) == 0)
    def _():
        state_ref[...] = jnp.zeros_like(state_ref)
        cbuf_ref[0:SUBLANES, :] = jnp.zeros((SUBLANES, cbuf_ref.shape[1]), F32)

    xbc = xbc_ref[...].astype(F32)
    cbuf_ref[SUBLANES:, :] = xbc
    cw = cw_ref[...]
    conv = cb_ref[...] + cw[CONV_K - 1:CONV_K, :] * xbc
    for k in range(CONV_K - 1):
        off = SUBLANES - (CONV_K - 1) + k
        conv = conv + cw[k:k + 1, :] * cbuf_ref[off:off + lb, :]
    cbuf_ref[0:SUBLANES, :] = xbc[lb - SUBLANES:, :]
    act = _silu(conv)

    a_neg = -jnp.exp(alog_ref[...])
    dtb = dtb_ref[...]
    dskip = dskip_ref[...]
    gain = g_ref[...]
    tri = (lax.broadcasted_iota(jnp.int32, (CHUNK, CHUNK), 0)
           >= lax.broadcasted_iota(jnp.int32, (CHUNK, CHUNK), 1))
    lane = lax.broadcasted_iota(jnp.int32, (CHUNK, LANES), 1)

    for c in range(lb // CHUNK):
        r0 = c * CHUNK
        xs = act[r0:r0 + CHUNK, 0:d_ssd]
        bm = act[r0:r0 + CHUNK, d_ssd:d_ssd + SSD_GROUPS * D_STATE].astype(BF16)
        cm = act[r0:r0 + CHUNK, d_ssd + SSD_GROUPS * D_STATE:].astype(BF16)
        dt = _softplus(sm_ref[r0:r0 + CHUNK, :] + dtb)
        acum = _cumsum_rows(dt * a_neg)
        acum_t = acum.T
        total = acum[CHUNK - 1:CHUNK, :]

        dt_e = _expand_heads(dt, CHUNK)
        in_decay_e = _expand_heads(jnp.exp(acum), CHUNK)
        out_decay_e = _expand_heads(jnp.exp(total - acum), CHUNK)
        chunk_decay_e = _expand_heads(jnp.exp(total), 1)

        xdt = xs * dt_e
        xdt_b = xdt.astype(BF16)
        w_state = (xdt * out_decay_e).astype(BF16)

        y_parts = []
        for g in range(SSD_GROUPS):
            bg = bm[:, g * D_STATE:(g + 1) * D_STATE]
            cg = cm[:, g * D_STATE:(g + 1) * D_STATE]
            cb = lax.dot_general(cg, bg, (((1,), (1,)), ((), ())),
                                 preferred_element_type=F32)
            st = state_ref[g]
            y_off = jnp.dot(cg, st.astype(BF16), preferred_element_type=F32)
            for pr in range(gw // LANES):
                pair = []
                col0 = g * gw + pr * LANES
                for hh in range(2):
                    h = (col0 // HEAD_DIM) + hh
                    seg = acum[:, h:h + 1] - acum_t[h:h + 1, :]
                    m = (cb * jnp.where(tri, jnp.exp(seg), 0.0)).astype(BF16)
                    pair.append(jnp.dot(m, xdt_b[:, col0:col0 + LANES],
                                        preferred_element_type=F32))
                y_parts.append(jnp.where(lane < HEAD_DIM, pair[0], pair[1])
                               + y_off[:, pr * LANES:(pr + 1) * LANES]
                               * in_decay_e[:, col0:col0 + LANES])
            upd = lax.dot_general(bg, w_state[:, g * gw:(g + 1) * gw], (((0,), (0,)), ((), ())),
                                  preferred_element_type=F32)
            state_ref[g] = st * chunk_decay_e[:, g * gw:(g + 1) * gw] + upd

        y = jnp.concatenate(y_parts, axis=1) + dskip * xs
        y = y * _silu(z_ref[r0:r0 + CHUNK, :].astype(F32))
        y_ref[r0:r0 + CHUNK, :] = _rms(y, gain).astype(BF16)


def _ssd(main, small, conv_w, conv_b, dtb, alog, dskip_e, gain, *, batch, seq, lb=512):
    t = batch * seq
    nj = seq // lb
    d_ssd = N_HEADS * HEAD_DIM
    d_conv = conv_w.shape[1]
    z_blk = d_conv // d_ssd
    vec = lambda n: pl.BlockSpec((1, n), lambda b, j: (0, 0))
    return pl.pallas_call(
        functools.partial(_ssd_kernel, lb=lb),
        out_shape=jax.ShapeDtypeStruct((t, d_ssd), BF16),
        grid=(batch, nj),
        in_specs=[pl.BlockSpec((lb, d_conv), lambda b, j: (b * nj + j, 0)),
                  pl.BlockSpec((lb, d_ssd), lambda b, j: (b * nj + j, z_blk)),
                  pl.BlockSpec((lb, LANES), lambda b, j: (b * nj + j, 0)),
                  pl.BlockSpec((CONV_K, d_conv), lambda b, j: (0, 0)),
                  vec(d_conv), vec(LANES), vec(LANES), vec(d_ssd), vec(d_ssd)],
        out_specs=pl.BlockSpec((lb, d_ssd), lambda b, j: (b * nj + j, 0)),
        scratch_shapes=[pltpu.VMEM((SSD_GROUPS, D_STATE, d_ssd // SSD_GROUPS), F32),
                        pltpu.VMEM((SUBLANES + lb, d_conv), F32)],
        compiler_params=_params("parallel", "arbitrary"),
        name="ssd",
    )(main, main, small, conv_w, conv_b, dtb, alog, dskip_e, gain)


def _attn_rows(project, fgb_ref, gq_ref, gk_ref, ones_ref, qa_ref, ka_ref, va_ref,
               carry_ref, ts):
    @pl.when(pl.program_id(1) == 0)
    def _():
        carry_ref[...] = jnp.zeros_like(carry_ref)

    def headnorm(v, g):
        ssq = jnp.dot((v * v).astype(BF16), ones_ref[...], preferred_element_type=F32)
        return v * lax.rsqrt(ssq * (1.0 / HEAD_DIM) + EPS) * g

    n_groups = 2
    gs = ts // n_groups
    lane = lax.broadcasted_iota(jnp.int32, (gs, LANES), 1)
    d = HEAD_DIM
    v_tail = jnp.where(lane == d, 1.0, 0.0)
    f_run = carry_ref[...]
    for grp in range(n_groups):
        rows = slice(grp * gs, (grp + 1) * gs)
        q, k, vg, sm = project(rows)
        qn = headnorm(q, gq_ref[...])
        kn = headnorm(k, gk_ref[...])

        xf = sm + fgb_ref[...]
        log_f = jnp.minimum(xf, 0.0) - jnp.log(1.0 + jnp.exp(-jnp.abs(xf)))
        fcum = _cumsum_rows(log_f) + f_run
        f_run = fcum[gs - 1:gs, :]
        f2 = fcum * LOG2E
        f_hi = f2.astype(BF16).astype(F32)
        rem = f2 - f_hi
        f_mid = rem.astype(BF16).astype(F32)
        f_lo = rem - f_mid

        for h in range(N_HEADS):
            j = h // 2
            qb = qn[:, j * LANES:(j + 1) * LANES]
            kb = kn[:, j * LANES:(j + 1) * LANES]
            vb = vg[:, j * LANES:(j + 1) * LANES]
            if h % 2:
                qb = pltpu.roll(qb, HEAD_DIM, 1)
                kb = pltpu.roll(kb, HEAD_DIM, 1)
                vb = pltpu.roll(vb, HEAD_DIM, 1)
            va_ref[0, h, rows, :] = jnp.where(lane < d, vb, v_tail).astype(BF16)
            fl = N_HEADS + h
            bh = jnp.broadcast_to(f_hi[:, fl:fl + 1], (gs, LANES))
            bm = jnp.broadcast_to(f_mid[:, fl:fl + 1], (gs, LANES))
            bl = jnp.broadcast_to(f_lo[:, fl:fl + 1], (gs, LANES))
            qx = jnp.where(lane == d, bh, jnp.where(lane == d + 1, bm, jnp.where(
                lane == d + 2, bl, jnp.where(lane < d + 6, 1.0, 0.0))))
            kx = jnp.where(lane < d + 3, 1.0, jnp.where(lane == d + 3, -bh, jnp.where(
                lane == d + 4, -bm, jnp.where(lane == d + 5, -bl, 0.0))))
            qa_ref[0, h, rows, :] = jnp.where(lane < d, qb, qx).astype(BF16)
            ka_ref[0, h, rows, :] = jnp.where(lane < d, kb, kx).astype(BF16)
    carry_ref[...] = f_run


def _flash_kernel(qa_ref, ka_ref, va_ref, o_ref, s_ref, m_ref, acc_ref, *, bq):
    i = pl.program_id(2)
    lane = lax.broadcasted_iota(jnp.int32, (bq, LANES), 1)
    tri = (lax.broadcasted_iota(jnp.int32, (bq, bq), 0)
           >= lax.broadcasted_iota(jnp.int32, (bq, bq), 1))

    def logits(kb, slot):
        k0 = pl.multiple_of(kb * bq, bq)
        for hh in range(2):
            s_ref[slot, hh] = lax.dot_general(
                qa_ref[0, hh], ka_ref[0, hh, pl.ds(k0, bq), :], (((1,), (1,)), ((), ())),
                preferred_element_type=F32)

    def consume(kb, slot, masked):
        k0 = pl.multiple_of(kb * bq, bq)
        for hh in range(2):
            s = s_ref[slot, hh]
            if masked:
                s = jnp.where(tri, s, -jnp.inf)
            m = m_ref[hh]
            m_new = jnp.maximum(m, jnp.max(s, axis=-1, keepdims=True))
            p = jnp.exp2((s - m_new).astype(BF16))
            acc_ref[hh] = (jnp.exp2(m - m_new) * acc_ref[hh]
                           + jnp.dot(p, va_ref[0, hh, pl.ds(k0, bq), :],
                                     preferred_element_type=F32))
            m_ref[hh] = m_new

    m_ref[...] = jnp.full(m_ref.shape, -jnp.inf, F32)
    acc_ref[...] = jnp.zeros(acc_ref.shape, F32)
    logits(0, 0)

    def pair(j, carry):
        kb = 2 * j
        logits(kb + 1, 1)
        consume(kb, 0, False)
        logits(kb + 2, 0)
        consume(kb + 1, 1, False)
        return carry

    lax.fori_loop(0, i // 2, pair, 0)

    @pl.when(i % 2 == 0)
    def _():
        consume(i, 0, True)

    @pl.when(i % 2 == 1)
    def _():
        logits(i, 1)
        consume(i - 1, 0, False)
        consume(i, 1, True)

    acc0 = acc_ref[0]
    acc1 = acc_ref[1]
    o0 = acc0 / acc0[:, HEAD_DIM:HEAD_DIM + 1]
    o1 = acc1 / acc1[:, HEAD_DIM:HEAD_DIM + 1]
    o_ref[...] = jnp.where(lane < HEAD_DIM, o0, pltpu.roll(o1, HEAD_DIM, 1)).astype(BF16)


def _flash(qa, ka, va, *, batch, seq, bq=512):
    nq = seq // bq
    npair = N_HEADS // 2
    whole = pl.BlockSpec((1, 2, seq, LANES), lambda b, hp, i: (b, hp, 0, 0))
    return pl.pallas_call(
        functools.partial(_flash_kernel, bq=bq),
        out_shape=jax.ShapeDtypeStruct((batch * seq, N_HEADS * HEAD_DIM), BF16),
        grid=(batch, npair, nq),
        in_specs=[pl.BlockSpec((1, 2, bq, LANES), lambda b, hp, i: (b, hp, i, 0)), whole, whole],
        out_specs=pl.BlockSpec((bq, LANES), lambda b, hp, i: (b * nq + i, hp)),
        scratch_shapes=[pltpu.VMEM((2, 2, bq, bq), F32),
                        pltpu.VMEM((2, bq, 1), F32),
                        pltpu.VMEM((2, bq, LANES), F32)],
        compiler_params=_params("parallel", "parallel", "arbitrary"),
        name="flash",
    )(qa, ka, va)


def _mix_residual(x_ref, ys_ref, ya_ref, ga_ref, wt_ref, wb_ref):
    ya = _rms(ya_ref[...].astype(F32), ga_ref[...]).astype(BF16)
    return (x_ref[...]
            + jnp.dot(ys_ref[...], wt_ref[...], preferred_element_type=F32)
            + jnp.dot(ya, wb_ref[...], preferred_element_type=F32))


def _ple_add(x, p, g_ref, wg_ref, wp_ref):
    u = _rms(x, g_ref[...]).astype(BF16)
    gate = 1.0 / (1.0 + jnp.exp(-jnp.dot(u, wg_ref[...], preferred_element_type=F32)))
    pe = jnp.dot(p.astype(BF16), wp_ref[...], preferred_element_type=F32)
    return x + gate * pe


def _const_spec(a):
    nd = a.ndim
    return pl.BlockSpec(a.shape, lambda *_: (0,) * nd, pipeline_mode=pl.Buffered(1))


def _split_rows(ref, x):
    n = x.shape[0]
    for s in range(SUBLANES):
        ref[pl.ds(s, n, stride=SUBLANES), :] = x[:, s * LANES:(s + 1) * LANES]


def _merge_rows(ref, r0=0, n=None):
    n = ref.shape[0] - r0 if n is None else n
    flat = ref.reshape(ref.shape[0] * SUBLANES, LANES)
    return jnp.concatenate(
        [flat[pl.ds(r0 * SUBLANES + s, n, stride=SUBLANES), :] for s in range(SUBLANES)], axis=1)


def _ffn_chunks(d_ff, step):
    return [(f0, min(step, d_ff - f0)) for f0 in range(0, d_ff, step)]


def _swiglu(u, wg, wu, wd, fc):
    y = None
    for f0, fw in _ffn_chunks(wg.shape[1], fc):
        hg = jnp.dot(u, wg[:, f0:f0 + fw], preferred_element_type=F32)
        hu = jnp.dot(u, wu[:, f0:f0 + fw], preferred_element_type=F32)
        h = (_silu(hg) * hu).astype(BF16)
        part = jnp.dot(h, wd[f0:f0 + fw, :], preferred_element_type=F32)
        y = part if y is None else y + part
    return y


def _dense_layer_kernel(x_ref, ys_ref, ya_ref, p_ref, ga_ref, wt_ref, wb_ref, g2_ref,
                        wg_ref, wu_ref, wd_ref, gp_ref, wpg_ref, wpp_ref, o_ref, *, fc):
    x1 = _mix_residual(x_ref, ys_ref, ya_ref, ga_ref, wt_ref, wb_ref)
    u = _rms(x1, g2_ref[...]).astype(BF16)
    x2 = x1 + _swiglu(u, wg_ref, wu_ref, wd_ref, fc)
    o_ref[...] = _ple_add(x2, p_ref[...], gp_ref, wpg_ref, wpp_ref)


def _dense_layer(x2d, y_ssd, y_att, p2d, ga, w_top, w_bot, g2, wg, wu, wd, gp, wpg, wpp,
                 *, tm=512, fc=1024):
    t, d = x2d.shape
    row = lambda a: pl.BlockSpec((tm, a.shape[1]), lambda i: (i, 0))
    consts = (ga, w_top, w_bot, g2, wg, wu, wd, gp, wpg, wpp)
    return pl.pallas_call(
        functools.partial(_dense_layer_kernel, fc=fc),
        out_shape=jax.ShapeDtypeStruct((t, d), F32),
        grid=(t // tm,),
        in_specs=[row(x2d), row(y_ssd), row(y_att), row(p2d)] + [_const_spec(a) for a in consts],
        out_specs=row(x2d),
        compiler_params=_params("parallel"),
        name="dense_layer",
    )(x2d, y_ssd, y_att, p2d, *consts)


def _route_kernel(x_ref, ys_ref, ya_ref, ga_ref, wt_ref, wb_ref, g2_ref, wrh_ref, wrl_ref,
                  x1_ref, u3_ref, meta_ref, mt_ref, cnt_ref, carry_ref, *, n_experts):
    tm = x_ref.shape[0]

    @pl.when(pl.program_id(0) == 0)
    def _():
        carry_ref[...] = jnp.zeros_like(carry_ref)

    x1 = _mix_residual(x_ref, ys_ref, ya_ref, ga_ref, wt_ref, wb_ref)
    x1_ref[...] = x1
    u = _rms(x1, g2_ref[...])
    _split_rows(u3_ref, u)

    n_groups = 2
    gm = tm // n_groups
    lane_i = lax.broadcasted_iota(jnp.int32, (gm, LANES), 1)
    lane = lane_i.astype(F32)
    count = carry_ref[...]
    for grp in range(n_groups):
        rows = slice(grp * gm, (grp + 1) * gm)
        ug = u[rows, :]
        u_hi = ug.astype(BF16)
        u_lo = (ug - u_hi.astype(F32)).astype(BF16)
        logits = (jnp.dot(u_hi, wrh_ref[...], preferred_element_type=F32)
                  + jnp.dot(u_lo, wrh_ref[...], preferred_element_type=F32)
                  + jnp.dot(u_hi, wrl_ref[...], preferred_element_type=F32))
        logits = jnp.where(lane_i < n_experts, logits, -jnp.inf)
        m1 = jnp.max(logits, axis=-1, keepdims=True)
        i1 = jnp.min(jnp.where(logits == m1, lane, float(LANES)), axis=-1, keepdims=True)
        rest = jnp.where(lane == i1, -jnp.inf, logits)
        m2 = jnp.max(rest, axis=-1, keepdims=True)
        i2 = jnp.min(jnp.where(rest == m2, lane, float(LANES)), axis=-1, keepdims=True)
        e2 = jnp.exp(m2 - m1)
        g1 = 1.0 / (1.0 + e2)
        g2 = e2 * g1

        hit = jnp.where(lane == i1, 1.0, jnp.where(lane == i2, 1.0, 0.0))
        incl = _cumsum_rows(hit)
        before = incl - hit + count
        r1 = jnp.sum(jnp.where(lane == i1, before, 0.0), axis=-1, keepdims=True)
        r2 = jnp.sum(jnp.where(lane == i2, before, 0.0), axis=-1, keepdims=True)
        meta = jnp.where(lane_i == 0, i1, jnp.where(lane_i == 1, i2, jnp.where(
            lane_i == 2, g1, jnp.where(lane_i == 3, g2, jnp.where(
                lane_i == 4, r1, jnp.where(lane_i == 5, r2, 0.0))))))
        meta_ref[rows, :] = meta
        mt_ref[:, rows] = meta.T[0:SUBLANES, :]
        count = count + incl[gm - 1:gm, :]
    carry_ref[...] = count
    cnt_ref[...] = count


def _route(x2d, y_ssd, y_att, ga, w_top, w_bot, g2, wr_hi, wr_lo, *, n_experts, tm):
    t, d = x2d.shape
    row = lambda a: pl.BlockSpec((tm, a.shape[1]), lambda i: (i, 0))
    consts = (ga, w_top, w_bot, g2, wr_hi, wr_lo)
    return pl.pallas_call(
        functools.partial(_route_kernel, n_experts=n_experts),
        out_shape=(jax.ShapeDtypeStruct((t, d), F32),
                   jax.ShapeDtypeStruct((t * SUBLANES, LANES), F32),
                   jax.ShapeDtypeStruct((t, LANES), F32),
                   jax.ShapeDtypeStruct((t // tm * SUBLANES, tm), F32),
                   jax.ShapeDtypeStruct((1, LANES), F32)),
        grid=(t // tm,),
        in_specs=[row(x2d), row(y_ssd), row(y_att)] + [_const_spec(a) for a in consts],
        out_specs=(row(x2d),
                   pl.BlockSpec((tm * SUBLANES, LANES), lambda i: (i, 0)),
                   pl.BlockSpec((tm, LANES), lambda i: (i, 0)),
                   pl.BlockSpec((SUBLANES, tm), lambda i: (i, 0)),
                   pl.BlockSpec((1, LANES), lambda i: (0, 0))),
        scratch_shapes=[pltpu.VMEM((1, LANES), F32)],
        compiler_params=_params("arbitrary"),
        name="route",
    )(x2d, y_ssd, y_att, *consts)


def _scatter_kernel(pad0_ref, padn_ref, nu_ref, pos_ref, u3_ref, xs_ref, zero_ref, sem,
                    *, n_experts, tr):
    n_sub = pos_ref.shape[0]
    tm = u3_ref.shape[0] // n_sub

    def row_copy(r, dst):
        return pltpu.make_async_copy(u3_ref.at[pl.ds(r, 1)], xs_ref.at[pl.ds(dst, 1)], sem)

    def issue(r, carry):
        for j in range(n_sub):
            for k in range(TOP_K):
                row_copy(j * tm + r, pos_ref[j, 0, k * tm + r]).start(priority=k)
        return carry

    lax.fori_loop(0, tm, issue, 0, unroll=8)

    def drain(r, carry):
        for _ in range(n_sub * TOP_K):
            row_copy(r, 0).wait()
        return carry

    lax.fori_loop(0, tm, drain, 0, unroll=8)

    @pl.when(pl.program_id(0) == pl.num_programs(0) - 1)
    def _():
        zero_ref[...] = jnp.zeros_like(zero_ref)

        def zero_row(dst):
            return pltpu.make_async_copy(zero_ref.at[pl.ds(0, 1)], xs_ref.at[pl.ds(dst, 1)], sem)

        def zero_tile(g):
            return pltpu.make_async_copy(zero_ref, xs_ref.at[pl.ds(g * tr, tr)], sem)

        def loop(lo, hi, fn):
            lax.fori_loop(lo, hi, lambda j, c: (fn(j), c)[1], 0)

        for e in range(n_experts):
            loop(0, padn_ref[e], lambda j: zero_row(pad0_ref[e] + j).start())
            loop(0, padn_ref[e], lambda j: zero_row(0).wait())
        n_tiles = xs_ref.shape[0] // tr
        loop(nu_ref[0], n_tiles, lambda g: zero_tile(g).start())
        loop(nu_ref[0], n_tiles, lambda g: zero_tile(0).wait())


def _scatter(pad0, padn, n_used, pos_blk, u3, *, n_tiles, tr, tm, n_sub=2):
    t = u3.shape[0]
    return pl.pallas_call(
        functools.partial(_scatter_kernel, n_experts=pad0.shape[0], tr=tr),
        out_shape=jax.ShapeDtypeStruct((n_tiles * tr, SUBLANES, LANES), F32),
        grid_spec=pltpu.PrefetchScalarGridSpec(
            num_scalar_prefetch=3,
            grid=(t // (n_sub * tm),),
            in_specs=[pl.BlockSpec((n_sub, 1, TOP_K * tm), lambda i, *_: (i, 0, 0),
                                   memory_space=pltpu.SMEM),
                      pl.BlockSpec((n_sub * tm, SUBLANES, LANES), lambda i, *_: (i, 0, 0))],
            out_specs=pl.BlockSpec(memory_space=pl.ANY),
            scratch_shapes=[pltpu.VMEM((tr, SUBLANES, LANES), F32), pltpu.SemaphoreType.DMA(())]),
        compiler_params=_params("arbitrary"),
        name="scatter_rows",
    )(pad0, padn, n_used, pos_blk, u3)


def _expert_kernel(te_ref, nu_ref, xs_ref, wg_ref, wu_ref, wd_ref, ys_ref, *, fc):
    used = pl.program_id(0) < nu_ref[0]

    @pl.when(used)
    def _():
        u = _merge_rows(xs_ref).astype(BF16)
        _split_rows(ys_ref, _swiglu(u, wg_ref.at[0], wu_ref.at[0], wd_ref.at[0], fc))

    @pl.when(jnp.logical_not(used))
    def _():
        ys_ref[...] = jnp.zeros_like(ys_ref)


def _experts(tile_expert, n_used, xs, wg, wu, wd, *, tr, fc=768):
    _, d, dff = wg.shape
    rows = pl.BlockSpec((tr * SUBLANES, LANES), lambda g, te, nu: (g, 0))
    return pl.pallas_call(
        functools.partial(_expert_kernel, fc=fc),
        out_shape=jax.ShapeDtypeStruct((xs.shape[0] * SUBLANES, LANES), F32),
        grid_spec=pltpu.PrefetchScalarGridSpec(
            num_scalar_prefetch=2,
            grid=(xs.shape[0] // tr,),
            in_specs=[pl.BlockSpec((tr, SUBLANES, LANES),
                                   lambda g, te, nu: (jnp.minimum(g, nu[0] - 1), 0, 0)),
                      pl.BlockSpec((1, d, dff), lambda g, te, nu: (te[g], 0, 0)),
                      pl.BlockSpec((1, d, dff), lambda g, te, nu: (te[g], 0, 0)),
                      pl.BlockSpec((1, dff, d), lambda g, te, nu: (te[g], 0, 0))],
            out_specs=rows),
        compiler_params=_params("arbitrary"),
        name="experts",
    )(tile_expert, n_used, xs, wg, wu, wd)


def _combine_kernel(pos_ref, nxt_ref, x1_ref, meta_ref, p_ref, gp_ref, wpg_ref, wpp_ref, ys_ref,
                    o_ref, ybuf_ref, sem):
    tm = x1_ref.shape[0]
    i = pl.program_id(0)
    slot = lax.rem(i, 2)

    def row_copy(sl, r, k, src):
        return pltpu.make_async_copy(ys_ref.at[pl.ds(src, 1)], ybuf_ref.at[sl, k, pl.ds(r, 1)],
                                     sem.at[sl])

    def request(rows_ref, sl):
        def issue(r, carry):
            for k in range(TOP_K):
                row_copy(sl, r, k, rows_ref[0, 0, k * tm + r]).start(priority=k)
            return carry
        lax.fori_loop(0, tm, issue, 0, unroll=8)

    @pl.when(i == 0)
    def _():
        request(pos_ref, 0)

    @pl.when(i + 1 < pl.num_programs(0))
    def _():
        request(nxt_ref, 1 - slot)

    def drain(r, carry):
        for k in range(TOP_K):
            row_copy(slot, r, k, 0).wait()
        return carry

    lax.fori_loop(0, tm, drain, 0, unroll=8)

    n_groups = 2
    gm = tm // n_groups
    for grp in range(n_groups):
        rows = slice(grp * gm, (grp + 1) * gm)
        meta = meta_ref[rows, :]
        x2 = (x1_ref[rows, :]
              + meta[:, 2:3] * _merge_rows(ybuf_ref.at[slot, 0], grp * gm, gm)
              + meta[:, 3:4] * _merge_rows(ybuf_ref.at[slot, 1], grp * gm, gm))
        o_ref[rows, :] = _ple_add(x2, p_ref[rows, :], gp_ref, wpg_ref, wpp_ref)


def _combine(pos_blk, x1, meta, p2d, gp, wpg, wpp, ys, *, tm):
    t, d = x1.shape
    nt = t // tm
    row = lambda a: pl.BlockSpec((tm, a.shape[1]), lambda i: (i, 0))
    rows_of = lambda f: pl.BlockSpec((1, 1, TOP_K * tm), lambda i: (f(i), 0, 0),
                                     memory_space=pltpu.SMEM)
    return pl.pallas_call(
        _combine_kernel,
        out_shape=jax.ShapeDtypeStruct((t, d), F32),
        grid=(nt,),
        in_specs=[rows_of(lambda i: i), rows_of(lambda i: jnp.minimum(i + 1, nt - 1)),
                  row(x1), row(meta), row(p2d), _const_spec(gp), _const_spec(wpg),
                  _const_spec(wpp), pl.BlockSpec(memory_space=pl.ANY)],
        out_specs=row(x1),
        scratch_shapes=[pltpu.VMEM((2, TOP_K, tm, SUBLANES, LANES), F32),
                        pltpu.SemaphoreType.DMA((2,))],
        compiler_params=_params("arbitrary"),
        name="combine_ple",
    )(pos_blk, pos_blk, x1, meta, p2d, gp, wpg, wpp, ys)


def _moe_layer(x2d, y_ssd, y_att, p2d, ga, w_top, w_bot, g2, wr_hi, wr_lo, wg, wu, wd,
               gp, wpg, wpp, *, tm=512, tr=512):
    t = x2d.shape[0]
    n_experts = wg.shape[0]
    x1, u3, meta, meta_t, cnt = _route(x2d, y_ssd, y_att, ga, w_top, w_bot, g2, wr_hi, wr_lo,
                               n_experts=n_experts, tm=tm)

    n_tiles = t * TOP_K // tr + n_experts
    counts = cnt[0, :n_experts].astype(jnp.int32)
    tiles_e = (counts + tr - 1) // tr
    ends = jnp.cumsum(tiles_e)
    row0 = (ends - tiles_e) * tr
    meta_t = meta_t.reshape(t // tm, SUBLANES, tm)
    expert = meta_t[:, :TOP_K].astype(jnp.int32)
    rank = meta_t[:, 2 * TOP_K:3 * TOP_K].astype(jnp.int32)
    pos_blk = (row0[expert] + rank).reshape(t // tm, 1, TOP_K * tm)
    n_used = ends[-1:].astype(jnp.int32)
    g = jnp.minimum(jnp.arange(n_tiles, dtype=jnp.int32), n_used - 1)
    tile_expert = jnp.sum(g[:, None] >= ends[None, :], axis=1).astype(jnp.int32)
    pad0 = (row0 + counts).astype(jnp.int32)
    padn = (tiles_e * tr - counts).astype(jnp.int32)

    as_tiles = lambda a: a.reshape(-1, SUBLANES, LANES)
    xs = _scatter(pad0, padn, n_used, pos_blk, as_tiles(u3), n_tiles=n_tiles, tr=tr, tm=tm)
    ys = _experts(tile_expert, n_used, xs, wg, wu, wd, tr=tr)
    return _combine(pos_blk, x1, meta, p2d, gp, wpg, wpp, as_tiles(ys), tm=tm)


def _pad_lanes(v, offset=0):
    out = jnp.zeros((1, LANES), F32)
    return out.at[0, offset:offset + v.shape[0]].set(v.astype(F32))


def kernel(x, p, norm1_g, w_in, conv_w, conv_b, dt_bias, a_log, d_skip, ssd_norm_g, fg_bias, q_norm_g, k_norm_g, attn_norm_g, w_out, norm2_g, w_gate_dense, w_up_dense, w_down_dense, w_router, w_gate_exp, w_up_exp, w_down_exp, ple_norm_g, w_ple_gate, w_ple_proj):
    batch, seq, d_model = x.shape
    depth = w_in.shape[0]
    t = batch * seq
    d_ssd = N_HEADS * HEAD_DIM
    d_att = N_HEADS * HEAD_DIM
    d_conv = conv_w.shape[2]
    n_experts = w_router.shape[2]
    o_xbc = d_ssd
    o_dt = o_xbc + d_conv
    o_q = o_dt + N_HEADS
    o_k = o_q + d_att
    o_v = o_k + d_att
    o_f = o_v + d_att

    ones_bd = (jnp.arange(d_att)[:, None] // HEAD_DIM
               == jnp.arange(d_att)[None, :] // HEAD_DIM).astype(BF16)
    row = lambda v: v.astype(F32).reshape(1, -1)

    xf = x.reshape(t, d_model)
    for i in range(depth):
        w = w_in[i]
        w_main = jnp.concatenate(
            [w[:, o_xbc:o_dt], w[:, :d_ssd], w[:, o_q:o_k], w[:, o_k:o_v], w[:, o_v:o_f]],
            axis=1).astype(BF16)
        w_small = jnp.zeros((d_model, LANES), F32)
        w_small = w_small.at[:, :N_HEADS].set(w[:, o_dt:o_q])
        w_small = w_small.at[:, N_HEADS:2 * N_HEADS].set(w[:, o_f:o_f + N_HEADS]).astype(BF16)

        gq = row(jnp.tile(q_norm_g[i], N_HEADS)) * (LOG2E * HEAD_DIM ** -0.5)
        gk = row(jnp.tile(k_norm_g[i], N_HEADS))
        main, small, qa, ka, va = _in_proj(
            xf, row(norm1_g[i]), w_main, w_small, _pad_lanes(fg_bias[i], N_HEADS), gq, gk,
            ones_bd, batch=batch, seq=seq)

        y_ssd = _ssd(main, small, conv_w[i].astype(F32), row(conv_b[i]),
                     _pad_lanes(dt_bias[i]), _pad_lanes(a_log[i]),
                     row(jnp.repeat(d_skip[i], HEAD_DIM)), row(ssd_norm_g[i]),
                     batch=batch, seq=seq)
        y_att = _flash(qa, ka, va, batch=batch, seq=seq)

        wo = w_out[i].astype(BF16)
        mix = (xf, y_ssd, y_att, p[i].reshape(t, -1), row(attn_norm_g[i]), wo[:d_ssd], wo[d_ssd:],
               row(norm2_g[i]))
        ple = (row(ple_norm_g[i]), w_ple_gate[i].astype(BF16), w_ple_proj[i].astype(BF16))
        j = i // 2
        if i % 2 == 0:
            xf = _dense_layer(*mix, w_gate_dense[j].astype(BF16), w_up_dense[j].astype(BF16),
                              w_down_dense[j].astype(BF16), *ple)
        else:
            wr = jnp.zeros((d_model, LANES), F32).at[:, :n_experts].set(w_router[j])
            wr_hi = wr.astype(BF16)
            wr_lo = (wr - wr_hi.astype(F32)).astype(BF16)
            xf = _moe_layer(*mix, wr_hi, wr_lo, w_gate_exp[j].astype(BF16),
                            w_up_exp[j].astype(BF16), w_down_exp[j].astype(BF16), *ple)
    return xf.reshape(batch, seq, d_model)
```

```python
import functools

import jax
import jax.numpy as jnp
from jax import lax
from jax.experimental import pallas as pl
from jax.experimental.pallas import tpu as pltpu

F32 = jnp.float32
BF16 = jnp.bfloat16

EPS = 1e-6
LOG2E = 1.4426950408889634
HEAD_DIM = 64
N_HEADS = 8
SSD_GROUPS = 2
D_STATE = 128
CONV_K = 4
CHUNK = 128
TOP_K = 2
LANES = 128
SUBLANES = 8
VMEM_LIMIT = 52 * 1024 * 1024


def _params(*sem):
    return pltpu.CompilerParams(dimension_semantics=sem, vmem_limit_bytes=VMEM_LIMIT)


def _rms(x, g):
    return x * lax.rsqrt(jnp.mean(x * x, axis=-1, keepdims=True) + EPS) * g


def _silu(x):
    return x / (1.0 + jnp.exp(-x))


def _softplus(x):
    return jnp.maximum(x, 0.0) + jnp.log(1.0 + jnp.exp(-jnp.abs(x)))


def _cumsum_rows(x):
    n = x.shape[0]
    row = lax.broadcasted_iota(jnp.int32, x.shape, 0)
    s = 1
    while s < n:
        x = x + jnp.where(row >= s, pltpu.roll(x, s, 0), 0.0)
        s *= 2
    return x


def _expand_heads(v, n_rows):
    lane = lax.broadcasted_iota(jnp.int32, (n_rows, LANES), 1)
    blocks = []
    for j in range(N_HEADS // 2):
        a = jnp.broadcast_to(v[:, 2 * j:2 * j + 1], (n_rows, LANES))
        b = jnp.broadcast_to(v[:, 2 * j + 1:2 * j + 2], (n_rows, LANES))
        blocks.append(jnp.where(lane < HEAD_DIM, a, b))
    return jnp.concatenate(blocks, axis=1)


def _in_proj_kernel(x_ref, g_ref, wm_ref, ws_ref, fgb_ref, gq_ref, gk_ref, ones_ref,
                    main_ref, small_ref, qa_ref, ka_ref, va_ref, carry_ref, *, tn):
    tm = x_ref.shape[0]
    d_att = N_HEADS * HEAD_DIM
    u = _rms(x_ref[...], g_ref[...]).astype(BF16)
    n_ssd = main_ref.shape[1]
    def project(rows):
        ug = u[rows, :]
        q, k, v = (jnp.dot(ug, wm_ref[:, n_ssd + c * d_att:n_ssd + (c + 1) * d_att],
                           preferred_element_type=F32) for c in range(3))
        sm = jnp.dot(ug, ws_ref[...], preferred_element_type=F32)
        small_ref[rows, :] = sm
        return q, k, v, sm

    _attn_rows(project, fgb_ref, gq_ref, gk_ref, ones_ref, qa_ref, ka_ref, va_ref,
               carry_ref, tm)
    for n0 in range(0, n_ssd, tn):
        main_ref[:, n0:n0 + tn] = jnp.dot(
            u, wm_ref[:, n0:n0 + tn], preferred_element_type=F32).astype(BF16)


def _in_proj(x2d, g, w_main, w_small, fgb, gq, gk, ones_bd, *, batch, seq, tm=512, tn=512):
    t, d = x2d.shape
    nj = seq // tm
    d_att = N_HEADS * HEAD_DIM
    n_ssd = w_main.shape[1] - 3 * d_att
    const = lambda a: pl.BlockSpec(a.shape, lambda b, j: (0, 0))
    rows = lambda n: pl.BlockSpec((tm, n), lambda b, j: (b * nj + j, 0))
    aug = jax.ShapeDtypeStruct((batch, N_HEADS, seq, LANES), BF16)
    aug_spec = pl.BlockSpec((1, N_HEADS, tm, LANES), lambda b, j: (b, 0, j, 0))
    return pl.pallas_call(
        functools.partial(_in_proj_kernel, tn=tn),
        out_shape=(jax.ShapeDtypeStruct((t, n_ssd), BF16),
                   jax.ShapeDtypeStruct((t, LANES), F32), aug, aug, aug),
        grid=(batch, nj),
        in_specs=[rows(d), const(g), const(w_main), const(w_small), const(fgb), const(gq),
                  const(gk), const(ones_bd)],
        out_specs=(rows(n_ssd), rows(LANES), aug_spec, aug_spec, aug_spec),
        scratch_shapes=[pltpu.VMEM((1, LANES), F32)],
        compiler_params=_params("parallel", "arbitrary"),
        name="in_proj",
    )(x2d, g, w_main, w_small, fgb, gq, gk, ones_bd)


def _ssd_kernel(xbc_ref, z_ref, sm_ref, cw_ref, cb_ref, dtb_ref, alog_ref, dskip_ref, g_ref,
                y_ref, state_ref, cbuf_ref, *, lb):
    d_ssd = N_HEADS * HEAD_DIM
    gw = d_ssd // SSD_GROUPS

    @pl.when(pl.program_id(1) == 0)
    def _():
        state_ref[...] = jnp.zeros_like(state_ref)
        cbuf_ref[0:SUBLANES, :] = jnp.zeros((SUBLANES, cbuf_ref.shape[1]), F32)

    xbc = xbc_ref[...].astype(F32)
    cbuf_ref[SUBLANES:, :] = xbc
    cw = cw_ref[...]
    conv = cb_ref[...] + cw[CONV_K - 1:CONV_K, :] * xbc
    for k in range(CONV_K - 1):
        off = SUBLANES - (CONV_K - 1) + k
        conv = conv + cw[k:k + 1, :] * cbuf_ref[off:off + lb, :]
    cbuf_ref[0:SUBLANES, :] = xbc[lb - SUBLANES:, :]
    act = _silu(conv)

    a_neg = -jnp.exp(alog_ref[...])
    dtb = dtb_ref[...]
    dskip = dskip_ref[...]
    gain = g_ref[...]
    tri = (lax.broadcasted_iota(jnp.int32, (CHUNK, CHUNK), 0)
           >= lax.broadcasted_iota(jnp.int32, (CHUNK, CHUNK), 1))
    lane = lax.broadcasted_iota(jnp.int32, (CHUNK, LANES), 1)

    for c in range(lb // CHUNK):
        r0 = c * CHUNK
        xs = act[r0:r0 + CHUNK, 0:d_ssd]
        bm = act[r0:r0 + CHUNK, d_ssd:d_ssd + SSD_GROUPS * D_STATE].astype(BF16)
        cm = act[r0:r0 + CHUNK, d_ssd + SSD_GROUPS * D_STATE:].astype(BF16)
        dt = _softplus(sm_ref[r0:r0 + CHUNK, :] + dtb)
        acum = _cumsum_rows(dt * a_neg)
        acum_t = acum.T
        total = acum[CHUNK - 1:CHUNK, :]

        dt_e = _expand_heads(dt, CHUNK)
        in_decay_e = _expand_heads(jnp.exp(acum), CHUNK)
        out_decay_e = _expand_heads(jnp.exp(total - acum), CHUNK)
        chunk_decay_e = _expand_heads(jnp.exp(total), 1)

        xdt = xs * dt_e
        xdt_b = xdt.astype(BF16)
        w_state = (xdt * out_decay_e).astype(BF16)

        y_parts = []
        for g in range(SSD_GROUPS):
            bg = bm[:, g * D_STATE:(g + 1) * D_STATE]
            cg = cm[:, g * D_STATE:(g + 1) * D_STATE]
            cb = lax.dot_general(cg, bg, (((1,), (1,)), ((), ())),
                                 preferred_element_type=F32)
            st = state_ref[g]
            y_off = jnp.dot(cg, st.astype(BF16), preferred_element_type=F32)
            for pr in range(gw // LANES):
                pair = []
                col0 = g * gw + pr * LANES
                for hh in range(2):
                    h = (col0 // HEAD_DIM) + hh
                    seg = acum[:, h:h + 1] - acum_t[h:h + 1, :]
                    m = (cb * jnp.where(tri, jnp.exp(seg), 0.0)).astype(BF16)
                    pair.append(jnp.dot(m, xdt_b[:, col0:col0 + LANES],
                                        preferred_element_type=F32))
                y_parts.append(jnp.where(lane < HEAD_DIM, pair[0], pair[1])
                               + y_off[:, pr * LANES:(pr + 1) * LANES]
                               * in_decay_e[:, col0:col0 + LANES])
            upd = lax.dot_general(bg, w_state[:, g * gw:(g + 1) * gw], (((0,), (0,)), ((), ())),
                                  preferred_element_type=F32)
            state_ref[g] = st * chunk_decay_e[:, g * gw:(g + 1) * gw] + upd

        y = jnp.concatenate(y_parts, axis=1) + dskip * xs
        y = y * _silu(z_ref[r0:r0 + CHUNK, :].astype(F32))
        y_ref[r0:r0 + CHUNK, :] = _rms(y, gain).astype(BF16)


def _ssd(main, small, conv_w, conv_b, dtb, alog, dskip_e, gain, *, batch, seq, lb=512):
    t = batch * seq
    nj = seq // lb
    d_ssd = N_HEADS * HEAD_DIM
    d_conv = conv_w.shape[1]
    z_blk = d_conv // d_ssd
    vec = lambda n: pl.BlockSpec((1, n), lambda b, j: (0, 0))
    return pl.pallas_call(
        functools.partial(_ssd_kernel, lb=lb),
        out_shape=jax.ShapeDtypeStruct((t, d_ssd), BF16),
        grid=(batch, nj),
        in_specs=[pl.BlockSpec((lb, d_conv), lambda b, j: (b * nj + j, 0)),
                  pl.BlockSpec((lb, d_ssd), lambda b, j: (b * nj + j, z_blk)),
                  pl.BlockSpec((lb, LANES), lambda b, j: (b * nj + j, 0)),
                  pl.BlockSpec((CONV_K, d_conv), lambda b, j: (0, 0)),
                  vec(d_conv), vec(LANES), vec(LANES), vec(d_ssd), vec(d_ssd)],
        out_specs=pl.BlockSpec((lb, d_ssd), lambda b, j: (b * nj + j, 0)),
        scratch_shapes=[pltpu.VMEM((SSD_GROUPS, D_STATE, d_ssd // SSD_GROUPS), F32),
                        pltpu.VMEM((SUBLANES + lb, d_conv), F32)],
        compiler_params=_params("parallel", "arbitrary"),
        name="ssd",
    )(main, main, small, conv_w, conv_b, dtb, alog, dskip_e, gain)


def _attn_rows(project, fgb_ref, gq_ref, gk_ref, ones_ref, qa_ref, ka_ref, va_ref,
               carry_ref, ts):
    @pl.when(pl.program_id(1) == 0)
    def _():
        carry_ref[...] = jnp.zeros_like(carry_ref)

    def headnorm(v, g):
        ssq = jnp.dot((v * v).astype(BF16), ones_ref[...], preferred_element_type=F32)
        return v * lax.rsqrt(ssq * (1.0 / HEAD_DIM) + EPS) * g

    n_groups = 2
    gs = ts // n_groups
    lane = lax.broadcasted_iota(jnp.int32, (gs, LANES), 1)
    d = HEAD_DIM
    v_tail = jnp.where(lane == d, 1.0, 0.0)
    f_run = carry_ref[...]
    for grp in range(n_groups):
        rows = slice(grp * gs, (grp + 1) * gs)
        q, k, vg, sm = project(rows)
        qn = headnorm(q, gq_ref[...])
        kn = headnorm(k, gk_ref[...])

        xf = sm + fgb_ref[...]
        log_f = jnp.minimum(xf, 0.0) - jnp.log(1.0 + jnp.exp(-jnp.abs(xf)))
        fcum = _cumsum_rows(log_f) + f_run
        f_run = fcum[gs - 1:gs, :]
        f2 = fcum * LOG2E
        f_hi = f2.astype(BF16).astype(F32)
        rem = f2 - f_hi
        f_mid = rem.astype(BF16).astype(F32)
        f_lo = rem - f_mid

        for h in range(N_HEADS):
            j = h // 2
            qb = qn[:, j * LANES:(j + 1) * LANES]
            kb = kn[:, j * LANES:(j + 1) * LANES]
            vb = vg[:, j * LANES:(j + 1) * LANES]
            if h % 2:
                qb = pltpu.roll(qb, HEAD_DIM, 1)
                kb = pltpu.roll(kb, HEAD_DIM, 1)
                vb = pltpu.roll(vb, HEAD_DIM, 1)
            va_ref[0, h, rows, :] = jnp.where(lane < d, vb, v_tail).astype(BF16)
            fl = N_HEADS + h
            bh = jnp.broadcast_to(f_hi[:, fl:fl + 1], (gs, LANES))
            bm = jnp.broadcast_to(f_mid[:, fl:fl + 1], (gs, LANES))
            bl = jnp.broadcast_to(f_lo[:, fl:fl + 1], (gs, LANES))
            qx = jnp.where(lane == d, bh, jnp.where(lane == d + 1, bm, jnp.where(
                lane == d + 2, bl, jnp.where(lane < d + 6, 1.0, 0.0))))
            kx = jnp.where(lane < d + 3, 1.0, jnp.where(lane == d + 3, -bh, jnp.where(
                lane == d + 4, -bm, jnp.where(lane == d + 5, -bl, 0.0))))
            qa_ref[0, h, rows, :] = jnp.where(lane < d, qb, qx).astype(BF16)
            ka_ref[0, h, rows, :] = jnp.where(lane < d, kb, kx).astype(BF16)
    carry_ref[...] = f_run


def _flash_kernel(qa_ref, ka_ref, va_ref, o_ref, s_ref, m_ref, acc_ref, *, bq):
    i = pl.program_id(2)
    nq = pl.num_programs(2)
    lane = lax.broadcasted_iota(jnp.int32, (bq, LANES), 1)
    tri = (lax.broadcasted_iota(jnp.int32, (bq, bq), 0)
           >= lax.broadcasted_iota(jnp.int32, (bq, bq), 1))

    def logits(kb, slot, qi=None):
        q0 = pl.multiple_of((i if qi is None else qi) * bq, bq)
        k0 = pl.multiple_of(kb * bq, bq)
        for hh in range(2):
            s_ref[slot, hh] = lax.dot_general(
                qa_ref[0, hh, pl.ds(q0, bq), :], ka_ref[0, hh, pl.ds(k0, bq), :],
                (((1,), (1,)), ((), ())), preferred_element_type=F32)

    def logits_of_next_tile():
        logits(0, 2, jnp.minimum(i + 1, nq - 1))

    def consume(kb, slot, masked):
        k0 = pl.multiple_of(kb * bq, bq)
        for hh in range(2):
            s = s_ref[slot, hh]
            if masked:
                s = jnp.where(tri, s, -jnp.inf)
            m = m_ref[hh]
            m_new = jnp.maximum(m, jnp.max(s, axis=-1, keepdims=True))
            p = jnp.exp2((s - m_new).astype(BF16))
            acc_ref[hh] = (jnp.exp2(m - m_new) * acc_ref[hh]
                           + jnp.dot(p, va_ref[0, hh, pl.ds(k0, bq), :],
                                     preferred_element_type=F32))
            m_ref[hh] = m_new

    m_ref[...] = jnp.full(m_ref.shape, -jnp.inf, F32)
    acc_ref[...] = jnp.zeros(acc_ref.shape, F32)

    @pl.when(i == 0)
    def _():
        logits(0, 0)

    @pl.when(i > 0)
    def _():
        s_ref[0] = s_ref[2]

    def pair(j, carry):
        kb = 2 * j
        logits(kb + 1, 1)
        consume(kb, 0, False)
        logits(kb + 2, 0)
        consume(kb + 1, 1, False)
        return carry

    lax.fori_loop(0, i // 2, pair, 0)

    @pl.when(i % 2 == 0)
    def _():
        logits_of_next_tile()
        consume(i, 0, True)

    @pl.when(i % 2 == 1)
    def _():
        logits(i, 1)
        logits_of_next_tile()
        consume(i - 1, 0, False)
        consume(i, 1, True)

    acc0 = acc_ref[0]
    acc1 = acc_ref[1]
    o0 = acc0 / acc0[:, HEAD_DIM:HEAD_DIM + 1]
    o1 = acc1 / acc1[:, HEAD_DIM:HEAD_DIM + 1]
    o_ref[...] = jnp.where(lane < HEAD_DIM, o0, pltpu.roll(o1, HEAD_DIM, 1)).astype(BF16)


def _flash(qa, ka, va, *, batch, seq, bq=512):
    nq = seq // bq
    npair = N_HEADS // 2
    whole = pl.BlockSpec((1, 2, seq, LANES), lambda b, hp, i: (b, hp, 0, 0))
    return pl.pallas_call(
        functools.partial(_flash_kernel, bq=bq),
        out_shape=jax.ShapeDtypeStruct((batch * seq, N_HEADS * HEAD_DIM), BF16),
        grid=(batch, npair, nq),
        in_specs=[whole, whole, whole],
        out_specs=pl.BlockSpec((bq, LANES), lambda b, hp, i: (b * nq + i, hp)),
        scratch_shapes=[pltpu.VMEM((3, 2, bq, bq), F32),
                        pltpu.VMEM((2, bq, 1), F32),
                        pltpu.VMEM((2, bq, LANES), F32)],
        compiler_params=_params("parallel", "parallel", "arbitrary"),
        name="flash",
    )(qa, ka, va)


def _mix_residual(x_ref, ys_ref, ya_ref, ga_ref, wt_ref, wb_ref):
    ya = _rms(ya_ref[...].astype(F32), ga_ref[...]).astype(BF16)
    return (x_ref[...]
            + jnp.dot(ys_ref[...], wt_ref[...], preferred_element_type=F32)
            + jnp.dot(ya, wb_ref[...], preferred_element_type=F32))


def _ple_add(x, p, g_ref, wg_ref, wp_ref):
    u = _rms(x, g_ref[...]).astype(BF16)
    gate = 1.0 / (1.0 + jnp.exp(-jnp.dot(u, wg_ref[...], preferred_element_type=F32)))
    pe = jnp.dot(p.astype(BF16), wp_ref[...], preferred_element_type=F32)
    return x + gate * pe


def _const_spec(a):
    nd = a.ndim
    return pl.BlockSpec(a.shape, lambda *_: (0,) * nd, pipeline_mode=pl.Buffered(1))


def _split_rows(ref, x):
    n = x.shape[0]
    for s in range(SUBLANES):
        ref[pl.ds(s, n, stride=SUBLANES), :] = x[:, s * LANES:(s + 1) * LANES]


def _merge_rows(ref, r0=0, n=None):
    n = ref.shape[0] - r0 if n is None else n
    flat = ref.reshape(ref.shape[0] * SUBLANES, LANES)
    return jnp.concatenate(
        [flat[pl.ds(r0 * SUBLANES + s, n, stride=SUBLANES), :] for s in range(SUBLANES)], axis=1)


def _ffn_chunks(d_ff, step):
    return [(f0, min(step, d_ff - f0)) for f0 in range(0, d_ff, step)]


def _swiglu(u, wg, wu, wd, fc):
    y = None
    for f0, fw in _ffn_chunks(wg.shape[1], fc):
        hg = jnp.dot(u, wg[:, f0:f0 + fw], preferred_element_type=F32)
        hu = jnp.dot(u, wu[:, f0:f0 + fw], preferred_element_type=F32)
        h = (_silu(hg) * hu).astype(BF16)
        part = jnp.dot(h, wd[f0:f0 + fw, :], preferred_element_type=F32)
        y = part if y is None else y + part
    return y


def _dense_layer_kernel(x_ref, ys_ref, ya_ref, p_ref, ga_ref, wt_ref, wb_ref, g2_ref,
                        wg_ref, wu_ref, wd_ref, gp_ref, wpg_ref, wpp_ref, o_ref, *, fc):
    x1 = _mix_residual(x_ref, ys_ref, ya_ref, ga_ref, wt_ref, wb_ref)
    u = _rms(x1, g2_ref[...]).astype(BF16)
    x2 = x1 + _swiglu(u, wg_ref, wu_ref, wd_ref, fc)
    o_ref[...] = _ple_add(x2, p_ref[...], gp_ref, wpg_ref, wpp_ref)


def _dense_layer(x2d, y_ssd, y_att, p2d, ga, w_top, w_bot, g2, wg, wu, wd, gp, wpg, wpp,
                 *, tm=512, fc=1024):
    t, d = x2d.shape
    row = lambda a: pl.BlockSpec((tm, a.shape[1]), lambda i: (i, 0))
    consts = (ga, w_top, w_bot, g2, wg, wu, wd, gp, wpg, wpp)
    return pl.pallas_call(
        functools.partial(_dense_layer_kernel, fc=fc),
        out_shape=jax.ShapeDtypeStruct((t, d), F32),
        grid=(t // tm,),
        in_specs=[row(x2d), row(y_ssd), row(y_att), row(p2d)] + [_const_spec(a) for a in consts],
        out_specs=row(x2d),
        compiler_params=_params("parallel"),
        name="dense_layer",
    )(x2d, y_ssd, y_att, p2d, *consts)


def _route_kernel(x_ref, ys_ref, ya_ref, ga_ref, wt_ref, wb_ref, g2_ref, wrh_ref, wrl_ref,
                  x1_ref, u3_ref, meta_ref, mt_ref, cnt_ref, carry_ref, *, n_experts):
    tm = x_ref.shape[0]

    @pl.when(pl.program_id(0) == 0)
    def _():
        carry_ref[...] = jnp.zeros_like(carry_ref)

    x1 = _mix_residual(x_ref, ys_ref, ya_ref, ga_ref, wt_ref, wb_ref)
    x1_ref[...] = x1
    u = _rms(x1, g2_ref[...])
    _split_rows(u3_ref, u)

    n_groups = 2
    gm = tm // n_groups
    lane_i = lax.broadcasted_iota(jnp.int32, (gm, LANES), 1)
    lane = lane_i.astype(F32)
    count = carry_ref[...]
    for grp in range(n_groups):
        rows = slice(grp * gm, (grp + 1) * gm)
        ug = u[rows, :]
        u_hi = ug.astype(BF16)
        u_lo = (ug - u_hi.astype(F32)).astype(BF16)
        logits = (jnp.dot(u_hi, wrh_ref[...], preferred_element_type=F32)
                  + jnp.dot(u_lo, wrh_ref[...], preferred_element_type=F32)
                  + jnp.dot(u_hi, wrl_ref[...], preferred_element_type=F32))
        logits = jnp.where(lane_i < n_experts, logits, -jnp.inf)
        m1 = jnp.max(logits, axis=-1, keepdims=True)
        i1 = jnp.min(jnp.where(logits == m1, lane, float(LANES)), axis=-1, keepdims=True)
        rest = jnp.where(lane == i1, -jnp.inf, logits)
        m2 = jnp.max(rest, axis=-1, keepdims=True)
        i2 = jnp.min(jnp.where(rest == m2, lane, float(LANES)), axis=-1, keepdims=True)
        e2 = jnp.exp(m2 - m1)
        g1 = 1.0 / (1.0 + e2)
        g2 = e2 * g1

        hit = jnp.where(lane == i1, 1.0, jnp.where(lane == i2, 1.0, 0.0))
        incl = _cumsum_rows(hit)
        before = incl - hit + count
        r1 = jnp.sum(jnp.where(lane == i1, before, 0.0), axis=-1, keepdims=True)
        r2 = jnp.sum(jnp.where(lane == i2, before, 0.0), axis=-1, keepdims=True)
        meta = jnp.where(lane_i == 0, i1, jnp.where(lane_i == 1, i2, jnp.where(
            lane_i == 2, g1, jnp.where(lane_i == 3, g2, jnp.where(
                lane_i == 4, r1, jnp.where(lane_i == 5, r2, 0.0))))))
        meta_ref[rows, :] = meta
        mt_ref[:, rows] = meta.T[0:SUBLANES, :]
        count = count + incl[gm - 1:gm, :]
    carry_ref[...] = count
    cnt_ref[...] = count


def _route(x2d, y_ssd, y_att, ga, w_top, w_bot, g2, wr_hi, wr_lo, *, n_experts, tm):
    t, d = x2d.shape
    row = lambda a: pl.BlockSpec((tm, a.shape[1]), lambda i: (i, 0))
    consts = (ga, w_top, w_bot, g2, wr_hi, wr_lo)
    return pl.pallas_call(
        functools.partial(_route_kernel, n_experts=n_experts),
        out_shape=(jax.ShapeDtypeStruct((t, d), F32),
                   jax.ShapeDtypeStruct((t * SUBLANES, LANES), F32),
                   jax.ShapeDtypeStruct((t, LANES), F32),
                   jax.ShapeDtypeStruct((t // tm * SUBLANES, tm), F32),
                   jax.ShapeDtypeStruct((1, LANES), F32)),
        grid=(t // tm,),
        in_specs=[row(x2d), row(y_ssd), row(y_att)] + [_const_spec(a) for a in consts],
        out_specs=(row(x2d),
                   pl.BlockSpec((tm * SUBLANES, LANES), lambda i: (i, 0)),
                   pl.BlockSpec((tm, LANES), lambda i: (i, 0)),
                   pl.BlockSpec((SUBLANES, tm), lambda i: (i, 0)),
                   pl.BlockSpec((1, LANES), lambda i: (0, 0))),
        scratch_shapes=[pltpu.VMEM((1, LANES), F32)],
        compiler_params=_params("arbitrary"),
        name="route",
    )(x2d, y_ssd, y_att, *consts)


def _scatter_kernel(pad0_ref, padn_ref, nu_ref, pos_ref, u3_ref, xs_ref, zero_ref, sem,
                    *, n_experts, tr):
    n_sub = pos_ref.shape[0]
    tm = u3_ref.shape[0] // n_sub

    def row_copy(r, dst):
        return pltpu.make_async_copy(u3_ref.at[pl.ds(r, 1)], xs_ref.at[pl.ds(dst, 1)], sem)

    def issue(r, carry):
        for j in range(n_sub):
            for k in range(TOP_K):
                row_copy(j * tm + r, pos_ref[j, 0, k * tm + r]).start(priority=k)
        return carry

    lax.fori_loop(0, tm, issue, 0, unroll=8)

    def drain(r, carry):
        for _ in range(n_sub * TOP_K):
            row_copy(r, 0).wait()
        return carry

    lax.fori_loop(0, tm, drain, 0, unroll=8)

    @pl.when(pl.program_id(0) == pl.num_programs(0) - 1)
    def _():
        zero_ref[...] = jnp.zeros_like(zero_ref)

        def zero_row(dst):
            return pltpu.make_async_copy(zero_ref.at[pl.ds(0, 1)], xs_ref.at[pl.ds(dst, 1)], sem)

        def zero_tile(g):
            return pltpu.make_async_copy(zero_ref, xs_ref.at[pl.ds(g * tr, tr)], sem)

        def loop(lo, hi, fn):
            lax.fori_loop(lo, hi, lambda j, c: (fn(j), c)[1], 0)

        for e in range(n_experts):
            loop(0, padn_ref[e], lambda j: zero_row(pad0_ref[e] + j).start())
            loop(0, padn_ref[e], lambda j: zero_row(0).wait())
        n_tiles = xs_ref.shape[0] // tr
        loop(nu_ref[0], n_tiles, lambda g: zero_tile(g).start())
        loop(nu_ref[0], n_tiles, lambda g: zero_tile(0).wait())


def _scatter(pad0, padn, n_used, pos_blk, u3, *, n_tiles, tr, tm, n_sub=2):
    t = u3.shape[0]
    return pl.pallas_call(
        functools.partial(_scatter_kernel, n_experts=pad0.shape[0], tr=tr),
        out_shape=jax.ShapeDtypeStruct((n_tiles * tr, SUBLANES, LANES), F32),
        grid_spec=pltpu.PrefetchScalarGridSpec(
            num_scalar_prefetch=3,
            grid=(t // (n_sub * tm),),
            in_specs=[pl.BlockSpec((n_sub, 1, TOP_K * tm), lambda i, *_: (i, 0, 0),
                                   memory_space=pltpu.SMEM),
                      pl.BlockSpec((n_sub * tm, SUBLANES, LANES), lambda i, *_: (i, 0, 0))],
            out_specs=pl.BlockSpec(memory_space=pl.ANY),
            scratch_shapes=[pltpu.VMEM((tr, SUBLANES, LANES), F32), pltpu.SemaphoreType.DMA(())]),
        compiler_params=_params("arbitrary"),
        name="scatter_rows",
    )(pad0, padn, n_used, pos_blk, u3)


def _expert_kernel(te_ref, nu_ref, xs_ref, wg_ref, wu_ref, wd_ref, ys_ref, *, fc):
    used = pl.program_id(0) < nu_ref[0]

    @pl.when(used)
    def _():
        u = _merge_rows(xs_ref).astype(BF16)
        _split_rows(ys_ref, _swiglu(u, wg_ref.at[0], wu_ref.at[0], wd_ref.at[0], fc))

    @pl.when(jnp.logical_not(used))
    def _():
        ys_ref[...] = jnp.zeros_like(ys_ref)


def _experts(tile_expert, n_used, xs, wg, wu, wd, *, tr, fc=768):
    _, d, dff = wg.shape
    rows = pl.BlockSpec((tr * SUBLANES, LANES), lambda g, te, nu: (g, 0))
    return pl.pallas_call(
        functools.partial(_expert_kernel, fc=fc),
        out_shape=jax.ShapeDtypeStruct((xs.shape[0] * SUBLANES, LANES), F32),
        grid_spec=pltpu.PrefetchScalarGridSpec(
            num_scalar_prefetch=2,
            grid=(xs.shape[0] // tr,),
            in_specs=[pl.BlockSpec((tr, SUBLANES, LANES),
                                   lambda g, te, nu: (jnp.minimum(g, nu[0] - 1), 0, 0)),
                      pl.BlockSpec((1, d, dff), lambda g, te, nu: (te[g], 0, 0)),
                      pl.BlockSpec((1, d, dff), lambda g, te, nu: (te[g], 0, 0)),
                      pl.BlockSpec((1, dff, d), lambda g, te, nu: (te[g], 0, 0))],
            out_specs=rows),
        compiler_params=_params("arbitrary"),
        name="experts",
    )(tile_expert, n_used, xs, wg, wu, wd)


def _combine_kernel(pos_ref, nxt_ref, x1_ref, meta_ref, p_ref, gp_ref, wpg_ref, wpp_ref, ys_ref,
                    o_ref, ybuf_ref, sem):
    tm = x1_ref.shape[0]
    i = pl.program_id(0)
    slot = lax.rem(i, 2)

    def row_copy(sl, r, k, src):
        return pltpu.make_async_copy(ys_ref.at[pl.ds(src, 1)], ybuf_ref.at[sl, k, pl.ds(r, 1)],
                                     sem.at[sl])

    def request(rows_ref, sl):
        def issue(r, carry):
            for k in range(TOP_K):
                row_copy(sl, r, k, rows_ref[0, 0, k * tm + r]).start(priority=k)
            return carry
        lax.fori_loop(0, tm, issue, 0, unroll=8)

    @pl.when(i == 0)
    def _():
        request(pos_ref, 0)

    @pl.when(i + 1 < pl.num_programs(0))
    def _():
        request(nxt_ref, 1 - slot)

    def drain(r, carry):
        for k in range(TOP_K):
            row_copy(slot, r, k, 0).wait()
        return carry

    lax.fori_loop(0, tm, drain, 0, unroll=8)

    n_groups = 2
    gm = tm // n_groups
    for grp in range(n_groups):
        rows = slice(grp * gm, (grp + 1) * gm)
        meta = meta_ref[rows, :]
        x2 = (x1_ref[rows, :]
              + meta[:, 2:3] * _merge_rows(ybuf_ref.at[slot, 0], grp * gm, gm)
              + meta[:, 3:4] * _merge_rows(ybuf_ref.at[slot, 1], grp * gm, gm))
        o_ref[rows, :] = _ple_add(x2, p_ref[rows, :], gp_ref, wpg_ref, wpp_ref)


def _combine(pos_blk, x1, meta, p2d, gp, wpg, wpp, ys, *, tm):
    t, d = x1.shape
    nt = t // tm
    row = lambda a: pl.BlockSpec((tm, a.shape[1]), lambda i: (i, 0))
    rows_of = lambda f: pl.BlockSpec((1, 1, TOP_K * tm), lambda i: (f(i), 0, 0),
                                     memory_space=pltpu.SMEM)
    return pl.pallas_call(
        _combine_kernel,
        out_shape=jax.ShapeDtypeStruct((t, d), F32),
        grid=(nt,),
        in_specs=[rows_of(lambda i: i), rows_of(lambda i: jnp.minimum(i + 1, nt - 1)),
                  row(x1), row(meta), row(p2d), _const_spec(gp), _const_spec(wpg),
                  _const_spec(wpp), pl.BlockSpec(memory_space=pl.ANY)],
        out_specs=row(x1),
        scratch_shapes=[pltpu.VMEM((2, TOP_K, tm, SUBLANES, LANES), F32),
                        pltpu.SemaphoreType.DMA((2,))],
        compiler_params=_params("arbitrary"),
        name="combine_ple",
    )(pos_blk, pos_blk, x1, meta, p2d, gp, wpg, wpp, ys)


def _moe_layer(x2d, y_ssd, y_att, p2d, ga, w_top, w_bot, g2, wr_hi, wr_lo, wg, wu, wd,
               gp, wpg, wpp, *, tm=512, tr=512):
    t = x2d.shape[0]
    n_experts = wg.shape[0]
    x1, u3, meta, meta_t, cnt = _route(x2d, y_ssd, y_att, ga, w_top, w_bot, g2, wr_hi, wr_lo,
                               n_experts=n_experts, tm=tm)

    n_tiles = t * TOP_K // tr + n_experts
    counts = cnt[0, :n_experts].astype(jnp.int32)
    tiles_e = (counts + tr - 1) // tr
    ends = jnp.cumsum(tiles_e)
    row0 = (ends - tiles_e) * tr
    meta_t = meta_t.reshape(t // tm, SUBLANES, tm)
    expert = meta_t[:, :TOP_K].astype(jnp.int32)
    rank = meta_t[:, 2 * TOP_K:3 * TOP_K].astype(jnp.int32)
    pos_blk = (row0[expert] + rank).reshape(t // tm, 1, TOP_K * tm)
    n_used = ends[-1:].astype(jnp.int32)
    g = jnp.minimum(jnp.arange(n_tiles, dtype=jnp.int32), n_used - 1)
    tile_expert = jnp.sum(g[:, None] >= ends[None, :], axis=1).astype(jnp.int32)
    pad0 = (row0 + counts).astype(jnp.int32)
    padn = (tiles_e * tr - counts).astype(jnp.int32)

    as_tiles = lambda a: a.reshape(-1, SUBLANES, LANES)
    xs = _scatter(pad0, padn, n_used, pos_blk, as_tiles(u3), n_tiles=n_tiles, tr=tr, tm=tm)
    ys = _experts(tile_expert, n_used, xs, wg, wu, wd, tr=tr)
    return _combine(pos_blk, x1, meta, p2d, gp, wpg, wpp, as_tiles(ys), tm=tm)


def _pad_lanes(v, offset=0):
    out = jnp.zeros((1, LANES), F32)
    return out.at[0, offset:offset + v.shape[0]].set(v.astype(F32))


def kernel(x, p, norm1_g, w_in, conv_w, conv_b, dt_bias, a_log, d_skip, ssd_norm_g, fg_bias, q_norm_g, k_norm_g, attn_norm_g, w_out, norm2_g, w_gate_dense, w_up_dense, w_down_dense, w_router, w_gate_exp, w_up_exp, w_down_exp, ple_norm_g, w_ple_gate, w_ple_proj):
    batch, seq, d_model = x.shape
    depth = w_in.shape[0]
    t = batch * seq
    d_ssd = N_HEADS * HEAD_DIM
    d_att = N_HEADS * HEAD_DIM
    d_conv = conv_w.shape[2]
    n_experts = w_router.shape[2]
    o_xbc = d_ssd
    o_dt = o_xbc + d_conv
    o_q = o_dt + N_HEADS
    o_k = o_q + d_att
    o_v = o_k + d_att
    o_f = o_v + d_att

    ones_bd = (jnp.arange(d_att)[:, None] // HEAD_DIM
               == jnp.arange(d_att)[None, :] // HEAD_DIM).astype(BF16)
    row = lambda v: v.astype(F32).reshape(1, -1)

    xf = x.reshape(t, d_model)
    for i in range(depth):
        w = w_in[i]
        w_main = jnp.concatenate(
            [w[:, o_xbc:o_dt], w[:, :d_ssd], w[:, o_q:o_k], w[:, o_k:o_v], w[:, o_v:o_f]],
            axis=1).astype(BF16)
        w_small = jnp.zeros((d_model, LANES), F32)
        w_small = w_small.at[:, :N_HEADS].set(w[:, o_dt:o_q])
        w_small = w_small.at[:, N_HEADS:2 * N_HEADS].set(w[:, o_f:o_f + N_HEADS]).astype(BF16)

        gq = row(jnp.tile(q_norm_g[i], N_HEADS)) * (LOG2E * HEAD_DIM ** -0.5)
        gk = row(jnp.tile(k_norm_g[i], N_HEADS))
        main, small, qa, ka, va = _in_proj(
            xf, row(norm1_g[i]), w_main, w_small, _pad_lanes(fg_bias[i], N_HEADS), gq, gk,
            ones_bd, batch=batch, seq=seq)

        y_ssd = _ssd(main, small, conv_w[i].astype(F32), row(conv_b[i]),
                     _pad_lanes(dt_bias[i]), _pad_lanes(a_log[i]),
                     row(jnp.repeat(d_skip[i], HEAD_DIM)), row(ssd_norm_g[i]),
                     batch=batch, seq=seq)
        y_att = _flash(qa, ka, va, batch=batch, seq=seq)

        wo = w_out[i].astype(BF16)
        mix = (xf, y_ssd, y_att, p[i].reshape(t, -1), row(attn_norm_g[i]), wo[:d_ssd], wo[d_ssd:],
               row(norm2_g[i]))
        ple = (row(ple_norm_g[i]), w_ple_gate[i].astype(BF16), w_ple_proj[i].astype(BF16))
        j = i // 2
        if i % 2 == 0:
            xf = _dense_layer(*mix, w_gate_dense[j].astype(BF16), w_up_dense[j].astype(BF16),
                              w_down_dense[j].astype(BF16), *ple)
        else:
            wr = jnp.zeros((d_model, LANES), F32).at[:, :n_experts].set(w_router[j])
            wr_hi = wr.astype(BF16)
            wr_lo = (wr - wr_hi.astype(F32)).astype(BF16)
            xf = _moe_layer(*mix, wr_hi, wr_lo, w_gate_exp[j].astype(BF16),
                            w_up_exp[j].astype(BF16), w_down_exp[j].astype(BF16), *ple)
    return xf.reshape(batch, seq, d_model)
```
